```python
import jax, jax.numpy as jnp
from jax import lax
import numpy as np

D_MODEL = 4096
BATCH = 1
SEQ = 16384
DEPTH = 2
DEC_BATCH = 8
DEC_SEQ = 64
PAST_LEN = 4096

CHUNK = 64
N_EVEN = (DEPTH + 1) // 2
N_ODD = DEPTH // 2
D_MIX = D_MODEL
D_POOL = D_MIX // 2
D_SCONV = D_MIX // 2
POOL_WINDOWS = (2, 4, 8, 16)
N_POOL_GROUPS = len(POOL_WINDOWS)
POOL_GROUP = D_POOL // N_POOL_GROUPS
POOL_HIST = max(POOL_WINDOWS) - 1
SCONV_WIDTH = 3
D_IN_AB = D_POOL + 3 * D_SCONV
D_CONF = D_MODEL
CONF_WIDTH = 31
D_FF = 11008
N_EXPERTS = 8
TOP_K = 2
D_FF_EXPERT = 14336
RMS_EPS = 1e-6
LN_EPS = 1e-5

kernel_name = "streaming_pool_shortconv_conformer_moe_trunk"


def rmsnorm(x, g):
    xf = x.astype(jnp.float32)
    y = xf * lax.rsqrt(jnp.mean(xf * xf, axis=-1, keepdims=True) + RMS_EPS)
    return (y * g.astype(jnp.float32)).astype(x.dtype)


def layernorm(x, g, b):
    xf = x.astype(jnp.float32)
    mu = jnp.mean(xf, axis=-1, keepdims=True)
    var = jnp.mean(jnp.square(xf - mu), axis=-1, keepdims=True)
    y = (xf - mu) * lax.rsqrt(var + LN_EPS)
    return (y * g.astype(jnp.float32) + b.astype(jnp.float32)).astype(x.dtype)


def modulate(h, shift, scale):
    return h * (1 + scale[:, None, :]) + shift[:, None, :]


def causal_dwconv(x_ext, w):
    width = w.shape[0]
    T = x_ext.shape[1] - (width - 1)
    y = x_ext[:, 0:T] * w[0]
    for k in range(1, width):
        y = y + x_ext[:, k:k + T] * w[k]
    return y


def pool_mixer(p_ext, pos0, w_grp, scale):
    B = p_ext.shape[0]
    T = p_ext.shape[1] - POOL_HIST
    pf = p_ext.astype(jnp.float32)
    cs = jnp.pad(jnp.cumsum(pf, axis=1), ((0, 0), (1, 0), (0, 0)))
    x_cur = pf[:, POOL_HIST:]
    pos = pos0 + jnp.arange(T)
    outs = []
    for gi, w in enumerate(POOL_WINDOWS):
        lo, hi = gi * POOL_GROUP, (gi + 1) * POOL_GROUP
        s = cs[:, POOL_HIST + 1:POOL_HIST + 1 + T, lo:hi] - cs[:, POOL_HIST + 1 - w:POOL_HIST + 1 - w + T, lo:hi]
        cnt = jnp.minimum(w, pos + 1).astype(jnp.float32)[None, :, None]
        outs.append(s / cnt - x_cur[..., lo:hi])
    d = jnp.stack(outs, axis=2).astype(p_ext.dtype)
    y = jnp.einsum('btgi,gio->btgo', d, w_grp)
    return y.reshape(B, T, D_POOL) * scale


def even_mixer(h, pos0, pool_hist, sconv_hist, w_in, w_grp, pool_scale, w_sconv, w_out):
    z = jnp.einsum('btd,de->bte', h, w_in)
    p = z[..., :D_POOL]
    bg, cg, v = jnp.split(z[..., D_POOL:], 3, axis=-1)
    p_ext = jnp.concatenate([pool_hist, p], axis=1)
    y_a = pool_mixer(p_ext, pos0, w_grp, pool_scale)
    u_ext = jnp.concatenate([sconv_hist, cg * v], axis=1)
    y_b = bg * causal_dwconv(u_ext, w_sconv)
    y = jnp.einsum('bte,ed->btd', jnp.concatenate([y_a, y_b], axis=-1), w_out)
    return y, p_ext[:, -POOL_HIST:], u_ext[:, -(SCONV_WIDTH - 1):]


def conformer_conv(h, conv_hist, w_pw1, b_pw1, w_dw, b_dw, ln_g, ln_b, w_pw2, b_pw2):
    z = jnp.einsum('btd,de->bte', h, w_pw1) + b_pw1
    a, g = jnp.split(z, 2, axis=-1)
    u = a * jax.nn.sigmoid(g)
    u_ext = jnp.concatenate([conv_hist, u], axis=1)
    v = causal_dwconv(u_ext, w_dw) + b_dw
    v = layernorm(v, ln_g, ln_b)
    v = v * jax.nn.sigmoid(v)
    y = jnp.einsum('bte,ed->btd', v, w_pw2) + b_pw2
    return y, u_ext[:, -(CONF_WIDTH - 1):]


def swiglu(h, w_gate, w_up, w_down):
    g = jnp.einsum('btd,df->btf', h, w_gate)
    u = jnp.einsum('btd,df->btf', h, w_up)
    return jnp.einsum('btf,fd->btd', jax.nn.silu(g) * u, w_down)


def moe_swiglu(h, w_router, w_gate, w_up, w_down):
    logits = jnp.einsum('btd,de->bte', h.astype(jnp.float32), w_router.astype(jnp.float32))
    top_v, top_i = lax.top_k(logits, TOP_K)
    top_w = jax.nn.softmax(top_v, axis=-1)
    combine = jnp.sum(jax.nn.one_hot(top_i, N_EXPERTS, dtype=jnp.float32) * top_w[..., None], axis=-2)
    combine = combine.astype(h.dtype)
    y = jnp.zeros_like(h)
    for e in range(N_EXPERTS):
        y = y + combine[..., e:e + 1] * swiglu(h, w_gate[e], w_up[e], w_down[e])
    return y


def trunk(x, c, pos0, pool_st, sconv_st, conf_st,
          w_ada, b_ada, g_mix, g_ffn, g_final,
          w_in_ab, w_pool_grp, pool_scale, w_sconv, w_out_ab,
          w_ffn_gate, w_ffn_up, w_ffn_down,
          w_pw1, b_pw1, w_dw, b_dw, ln_g, ln_b, w_pw2, b_pw2,
          w_router, w_exp_gate, w_exp_up, w_exp_down):
    new_pool, new_sconv, new_conf = [], [], []
    for l in range(DEPTH):
        mod = jnp.einsum('bd,de->be', jax.nn.silu(c), w_ada[l]) + b_ada[l]
        sh_m, sc_m, gt_m, sh_f, sc_f, gt_f = jnp.split(mod, 6, axis=-1)
        h = modulate(rmsnorm(x, g_mix[l]), sh_m, sc_m)
        i = l // 2
        if l % 2 == 0:
            y, ps, ss = even_mixer(h, pos0, pool_st[i], sconv_st[i], w_in_ab[i], w_pool_grp[i],
                                   pool_scale[i], w_sconv[i], w_out_ab[i])
            new_pool.append(ps)
            new_sconv.append(ss)
        else:
            y, cs = conformer_conv(h, conf_st[i], w_pw1[i], b_pw1[i], w_dw[i], b_dw[i],
                                   ln_g[i], ln_b[i], w_pw2[i], b_pw2[i])
            new_conf.append(cs)
        x = x + gt_m[:, None, :] * y
        h = modulate(rmsnorm(x, g_ffn[l]), sh_f, sc_f)
        if l % 2 == 0:
            f = swiglu(h, w_ffn_gate[i], w_ffn_up[i], w_ffn_down[i])
        else:
            f = moe_swiglu(h, w_router[i], w_exp_gate[i], w_exp_up[i], w_exp_down[i])
        x = x + gt_f[:, None, :] * f
    return rmsnorm(x, g_final), jnp.stack(new_pool), jnp.stack(new_sconv), jnp.stack(new_conf)


def setup_inputs(seed: int = 0) -> dict:
    key = jax.random.key(seed)
    ks = iter(jax.random.split(key, 40))
    nrm = lambda shape, s: jax.random.normal(next(ks), shape, jnp.float32) * s
    d = D_MODEL
    inp = {}
    inp['x_prompt'] = nrm((BATCH, SEQ, d), 1.0)
    inp['x_sample'] = nrm((DEC_BATCH, DEC_SEQ, d), 1.0)
    inp['state_pool'] = nrm((N_EVEN, DEC_BATCH, POOL_HIST, D_POOL), 1.0)
    inp['state_shortconv'] = nrm((N_EVEN, DEC_BATCH, SCONV_WIDTH - 1, D_SCONV), 1.0)
    inp['state_dwconv'] = nrm((N_ODD, DEC_BATCH, CONF_WIDTH - 1, D_CONF), 0.5)
    inp['c_prompt'] = nrm((BATCH, d), 1.0)
    inp['c_sample'] = nrm((DEC_BATCH, d), 1.0)
    inp['w_ada'] = nrm((DEPTH, d, 6 * d), 0.5 * d ** -0.5)
    inp['b_ada'] = nrm((DEPTH, 6 * d), 0.02)
    inp['g_mix'] = 1.0 + nrm((DEPTH, d), 0.01)
    inp['g_ffn'] = 1.0 + nrm((DEPTH, d), 0.01)
    inp['g_final'] = 1.0 + nrm((d,), 0.01)
    inp['w_in_ab'] = nrm((N_EVEN, d, D_IN_AB), d ** -0.5)
    inp['w_pool_grp'] = nrm((N_EVEN, N_POOL_GROUPS, POOL_GROUP, POOL_GROUP), POOL_GROUP ** -0.5)
    inp['pool_scale'] = 1.0 + nrm((N_EVEN, D_POOL), 0.1)
    inp['w_sconv'] = nrm((N_EVEN, SCONV_WIDTH, D_SCONV), SCONV_WIDTH ** -0.5)
    inp['w_out_ab'] = nrm((N_EVEN, D_MIX, d), D_MIX ** -0.5)
    inp['w_ffn_gate'] = nrm((N_EVEN, d, D_FF), d ** -0.5)
    inp['w_ffn_up'] = nrm((N_EVEN, d, D_FF), d ** -0.5)
    inp['w_ffn_down'] = nrm((N_EVEN, D_FF, d), D_FF ** -0.5)
    inp['w_pw1'] = nrm((N_ODD, d, 2 * D_CONF), d ** -0.5)
    inp['b_pw1'] = nrm((N_ODD, 2 * D_CONF), 0.02)
    inp['w_dw'] = nrm((N_ODD, CONF_WIDTH, D_CONF), CONF_WIDTH ** -0.5)
    inp['b_dw'] = nrm((N_ODD, D_CONF), 0.02)
    inp['ln_g'] = 1.0 + nrm((N_ODD, D_CONF), 0.01)
    inp['ln_b'] = nrm((N_ODD, D_CONF), 0.02)
    inp['w_pw2'] = nrm((N_ODD, D_CONF, d), D_CONF ** -0.5)
    inp['b_pw2'] = nrm((N_ODD, d), 0.02)
    inp['w_router'] = nrm((N_ODD, d, N_EXPERTS), d ** -0.5)
    inp['w_exp_gate'] = nrm((N_ODD, N_EXPERTS, d, D_FF_EXPERT), d ** -0.5)
    inp['w_exp_up'] = nrm((N_ODD, N_EXPERTS, d, D_FF_EXPERT), d ** -0.5)
    inp['w_exp_down'] = nrm((N_ODD, N_EXPERTS, D_FF_EXPERT, d), D_FF_EXPERT ** -0.5)
    return inp


def reference(x_prompt, x_sample, state_pool, state_shortconv, state_dwconv, c_prompt, c_sample,
              w_ada, b_ada, g_mix, g_ffn, g_final,
              w_in_ab, w_pool_grp, pool_scale, w_sconv, w_out_ab,
              w_ffn_gate, w_ffn_up, w_ffn_down,
              w_pw1, b_pw1, w_dw, b_dw, ln_g, ln_b, w_pw2, b_pw2,
              w_router, w_exp_gate, w_exp_up, w_exp_down):
    dt = x_prompt.dtype
    nb = x_prompt.shape[0]
    pool0 = jnp.zeros((N_EVEN, nb, POOL_HIST, D_POOL), dt)
    sconv0 = jnp.zeros((N_EVEN, nb, SCONV_WIDTH - 1, D_SCONV), dt)
    conf0 = jnp.zeros((N_ODD, nb, CONF_WIDTH - 1, D_CONF), dt)
    y_prompt, new_pool_p, new_shortconv_p, new_dwconv_p = trunk(
        x_prompt, c_prompt, 0, pool0, sconv0, conf0,
        w_ada, b_ada, g_mix, g_ffn, g_final,
        w_in_ab, w_pool_grp, pool_scale, w_sconv, w_out_ab,
        w_ffn_gate, w_ffn_up, w_ffn_down,
        w_pw1, b_pw1, w_dw, b_dw, ln_g, ln_b, w_pw2, b_pw2,
        w_router, w_exp_gate, w_exp_up, w_exp_down)
    y_sample, new_pool_s, new_shortconv_s, new_dwconv_s = trunk(
        x_sample, c_sample, PAST_LEN, state_pool, state_shortconv, state_dwconv,
        w_ada, b_ada, g_mix, g_ffn, g_final,
        w_in_ab, w_pool_grp, pool_scale, w_sconv, w_out_ab,
        w_ffn_gate, w_ffn_up, w_ffn_down,
        w_pw1, b_pw1, w_dw, b_dw, ln_g, ln_b, w_pw2, b_pw2,
        w_router, w_exp_gate, w_exp_up, w_exp_down)
    return (y_prompt, y_sample, new_pool_p, new_shortconv_p, new_dwconv_p, new_pool_s, new_shortconv_s, new_dwconv_s)
```

```python
import functools

import jax
import jax.numpy as jnp
from jax import lax
from jax.experimental import pallas as pl
from jax.experimental.pallas import tpu as pltpu

F32 = jnp.float32
BF16 = jnp.bfloat16

CHUNK = 64
POOL_WINDOWS = (2, 4, 8, 16)
POOL_HIST = max(POOL_WINDOWS) - 1
POOL_HIST_PAD = 16
SCONV_HIST_PAD = 8
DWCONV_HIST_PAD = 32
PAST_LEN = 4096
TOP_K = 2
RMS_EPS = 1e-6
LN_EPS = 1e-5
MOD_ROWS = 16
ROUTE_LANES = 128
V7X_VMEM_BYTES = 64 * 1024 * 1024
VMEM_LIMIT = V7X_VMEM_BYTES - 6 * 1024 * 1024


def _params(*sem):
    return pltpu.CompilerParams(dimension_semantics=sem, vmem_limit_bytes=VMEM_LIMIT)


def _pick(n, prefs):
    for p in prefs:
        if n % p == 0:
            return p
    return n


def _mod_body(c_ref, w_ref, b_ref, o_ref):
    c = c_ref[...]
    sc = (c * jax.nn.sigmoid(c)).astype(BF16)
    o_ref[...] = jnp.dot(sc, w_ref[...].astype(BF16), preferred_element_type=F32) + b_ref[...]


def _mod_tables(c_all, w_ada, b_ada):
    depth, d, n = w_ada.shape
    tn = _pick(n, (512, 256, 128))
    return pl.pallas_call(
        _mod_body,
        out_shape=jax.ShapeDtypeStruct((depth, MOD_ROWS, n), F32),
        grid=(depth, n // tn),
        in_specs=[
            pl.BlockSpec((MOD_ROWS, d), lambda l, j: (0, 0)),
            pl.BlockSpec((None, d, tn), lambda l, j: (l, 0, j)),
            pl.BlockSpec((None, 1, tn), lambda l, j: (l, 0, j)),
        ],
        out_specs=pl.BlockSpec((None, MOD_ROWS, tn), lambda l, j: (l, 0, j)),
        compiler_params=_params("arbitrary", "arbitrary"),
        name="mod_tables",
    )(c_all, w_ada, b_ada.reshape(depth, 1, n))


def _chunk_batch(chunk_idx, n_prompt_chunks):
    return jnp.maximum(chunk_idx - (n_prompt_chunks - 1), 0)


def _addnorm_body(*refs, tm, n_prompt_chunks, has_add, has_mod):
    refs = list(refs)
    x_ref = refs.pop(0)
    if has_add:
        y_ref = refs.pop(0)
        gate_ref = refs.pop(0)
    g_ref = refs.pop(0)
    if has_mod:
        shift_ref = refs.pop(0)
        scale_ref = refs.pop(0)
    if has_add:
        xo_ref = refs.pop(0)
    h_refs = refs
    i = pl.program_id(0)
    for c in range(tm // CHUNK):
        rows = pl.ds(c * CHUNK, CHUNK)
        b = _chunk_batch(i * (tm // CHUNK) + c, n_prompt_chunks)
        xv = x_ref[rows, :]
        if has_add:
            xv = xv + gate_ref[pl.ds(b, 1), :] * y_ref[rows, :]
            xo_ref[rows, :] = xv
        ms = jnp.mean(xv * xv, axis=-1, keepdims=True)
        hv = xv * lax.rsqrt(ms + RMS_EPS) * g_ref[...]
        if has_mod:
            hv = hv * (1.0 + scale_ref[pl.ds(b, 1), :]) + shift_ref[pl.ds(b, 1), :]
        for h_ref in h_refs:
            h_ref[rows, :] = hv.astype(h_ref.dtype)


def _addnorm(x, g, n_prompt_chunks, mod, *, y=None, gate=None, shift=None, scale=None,
             h_dtypes=(BF16,)):
    t, d = x.shape
    tm = _pick(t, (256, 128, 64))
    has_add = y is not None
    has_mod = shift is not None
    row = pl.BlockSpec((tm, d), lambda i: (i, 0))

    def tab(where):
        layer, col = where
        return pl.BlockSpec((None, MOD_ROWS, d), lambda i: (layer, 0, col))

    args, specs = [x], [row]
    if has_add:
        args += [y, mod]
        specs += [row, tab(gate)]
    args.append(g.reshape(1, d))
    specs.append(pl.BlockSpec((1, d), lambda i: (0, 0)))
    if has_mod:
        args += [mod, mod]
        specs += [tab(shift), tab(scale)]
    out_shape, out_specs = [], []
    if has_add:
        out_shape.append(jax.ShapeDtypeStruct((t, d), F32))
        out_specs.append(row)
    for dt in h_dtypes:
        out_shape.append(jax.ShapeDtypeStruct((t, d), dt))
        out_specs.append(row)
    body = functools.partial(_addnorm_body, tm=tm, n_prompt_chunks=n_prompt_chunks,
                             has_add=has_add, has_mod=has_mod)
    return pl.pallas_call(
        body, out_shape=out_shape, grid=(t // tm,), in_specs=specs, out_specs=out_specs,
        compiler_params=_params("arbitrary"), name="addnorm",
    )(*args)


def _mm_body(a_ref, b_ref, *refs, has_bias):
    o_ref = refs[-1]
    acc = jnp.dot(a_ref[...], b_ref[...], preferred_element_type=F32)
    if has_bias:
        acc = acc + refs[0][...]
    o_ref[...] = acc


def _matmul(a, b, bias=None):
    t, k = a.shape
    n = b.shape[1]
    tm = _pick(t, (768, 512, 256, 128, 64))
    tn = _pick(n, (1024, 512, 256, 128))
    args = [a, b]
    specs = [pl.BlockSpec((tm, k), lambda i, j: (i, 0)), pl.BlockSpec((k, tn), lambda i, j: (0, j))]
    if bias is not None:
        args.append(bias.reshape(1, n))
        specs.append(pl.BlockSpec((1, tn), lambda i, j: (0, j)))
    return pl.pallas_call(
        functools.partial(_mm_body, has_bias=bias is not None),
        out_shape=jax.ShapeDtypeStruct((t, n), F32),
        grid=(t // tm, n // tn), in_specs=specs,
        out_specs=pl.BlockSpec((tm, tn), lambda i, j: (i, j)),
        compiler_params=_params("arbitrary", "arbitrary"), name="matmul",
    )(*args)


def _glu_body(a_ref, ba_ref, bg_ref, bias_a_ref, bias_g_ref, o_ref):
    a = a_ref[...]
    za = jnp.dot(a, ba_ref[...], preferred_element_type=F32) + bias_a_ref[...]
    zg = jnp.dot(a, bg_ref[...], preferred_element_type=F32) + bias_g_ref[...]
    o_ref[...] = za * jax.nn.sigmoid(zg)


def _matmul_glu(a, b, bias):
    t, k = a.shape
    n = b.shape[1] // 2
    tm = _pick(t, (768, 512, 256, 128, 64))
    tn = _pick(n, (512, 256, 128))
    nb = n // tn
    bias = bias.reshape(1, 2 * n)
    return pl.pallas_call(
        _glu_body,
        out_shape=jax.ShapeDtypeStruct((t, n), F32),
        grid=(t // tm, nb),
        in_specs=[
            pl.BlockSpec((tm, k), lambda i, j: (i, 0)),
            pl.BlockSpec((k, tn), lambda i, j: (0, j)),
            pl.BlockSpec((k, tn), lambda i, j: (0, j + nb)),
            pl.BlockSpec((1, tn), lambda i, j: (0, j)),
            pl.BlockSpec((1, tn), lambda i, j: (0, j + nb)),
        ],
        out_specs=pl.BlockSpec((tm, tn), lambda i, j: (i, j)),
        compiler_params=_params("arbitrary", "arbitrary"), name="matmul_glu",
    )(a, b, b, bias, bias)


def _seq0_compute(pos_base, ph, uh, p_ref, bg_ref, cg_ref, v_ref, wgrp_ref, scale_ref, wsc_ref,
                  o_ref, utail_ref, pext, uext, *, tm, d_pool):
    n_groups = len(POOL_WINDOWS)
    gw = d_pool // n_groups
    pext[pl.ds(0, POOL_HIST_PAD), :] = ph
    pext[pl.ds(POOL_HIST_PAD, tm), :] = p_ref[...]
    u = cg_ref[...] * v_ref[...]
    uext[pl.ds(0, SCONV_HIST_PAD), :] = uh
    uext[pl.ds(SCONV_HIST_PAD, tm), :] = u
    utail_ref[...] = u[tm - SCONV_HIST_PAD:, :]

    pos = pos_base + lax.broadcasted_iota(jnp.int32, (tm, 1), 0)
    for gi, w in enumerate(POOL_WINDOWS):
        cols = pl.ds(gi * gw, gw)
        cur = pext[pl.ds(POOL_HIST_PAD, tm), cols]
        s = cur
        for k in range(1, w):
            s = s + pext[pl.ds(POOL_HIST_PAD - k, tm), cols]
        cnt = jnp.minimum(w, pos + 1).astype(F32)
        dlt = (s / cnt - cur).astype(BF16)
        ya = jnp.dot(dlt, wgrp_ref[gi], preferred_element_type=F32) * scale_ref[:, cols]
        o_ref[:, cols] = ya.astype(o_ref.dtype)

    conv = (uext[pl.ds(SCONV_HIST_PAD - 2, tm), :] * wsc_ref[pl.ds(0, 1), :]
            + uext[pl.ds(SCONV_HIST_PAD - 1, tm), :] * wsc_ref[pl.ds(1, 1), :]
            + uext[pl.ds(SCONV_HIST_PAD, tm), :] * wsc_ref[pl.ds(2, 1), :])
    o_ref[:, pl.ds(d_pool, d_pool)] = (bg_ref[...] * conv).astype(o_ref.dtype)


def _seq0_prompt_body(p_ref, bg_ref, cg_ref, v_ref, ph_ref, cgh_ref, vh_ref, wgrp_ref, scale_ref,
                      wsc_ref, o_ref, utail_ref, pext, uext, *, tm, d_pool, n_tiles):
    i = pl.program_id(0)

    @pl.when(i < n_tiles)
    def _():
        live = (i > 0).astype(F32)
        ph = ph_ref[...] * live
        uh = cgh_ref[...] * vh_ref[...] * live
        _seq0_compute(i * tm, ph, uh, p_ref, bg_ref, cg_ref, v_ref, wgrp_ref, scale_ref, wsc_ref,
                      o_ref, utail_ref, pext, uext, tm=tm, d_pool=d_pool)

    @pl.when(i >= n_tiles)
    def _():
        o_ref[...] = jnp.zeros_like(o_ref)


def _seq0_sample_body(p_ref, bg_ref, cg_ref, v_ref, ph_ref, uh_ref, wgrp_ref, scale_ref, wsc_ref,
                      prev_ref, o_ref, utail_ref, pext, uext, *, tm, d_pool):
    del prev_ref
    _seq0_compute(PAST_LEN, ph_ref[...], uh_ref[...], p_ref, bg_ref, cg_ref, v_ref, wgrp_ref,
                  scale_ref, wsc_ref, o_ref, utail_ref, pext, uext, tm=tm, d_pool=d_pool)


def _seq0(z, n_prompt, state_pool, state_sconv, w_grp, pool_scale, w_sconv):
    t = z.shape[0]
    n_batch, _, d_pool = state_pool.shape
    tm = _pick(n_prompt, (256, 128, 64))
    assert t % tm == 0
    n_tiles = n_prompt // tm
    scratch = lambda rows: [pltpu.VMEM((POOL_HIST_PAD + rows, d_pool), F32),
                            pltpu.VMEM((SCONV_HIST_PAD + rows, d_pool), F32)]
    consts = [w_grp, pool_scale.reshape(1, d_pool), w_sconv]
    const_specs = [pl.BlockSpec(w_grp.shape, lambda i: (0, 0, 0)),
                   pl.BlockSpec((1, d_pool), lambda i: (0, 0)),
                   pl.BlockSpec(w_sconv.shape, lambda i: (0, 0))]
    col = lambda c: pl.BlockSpec((tm, d_pool), lambda i: (i, c))
    hist = lambda rows, c: pl.BlockSpec(
        (rows, d_pool), lambda i: (jnp.maximum(i * (tm // rows) - 1, 0), c))
    y_p, utail_p = pl.pallas_call(
        functools.partial(_seq0_prompt_body, tm=tm, d_pool=d_pool, n_tiles=n_tiles),
        out_shape=[jax.ShapeDtypeStruct((t, 2 * d_pool), BF16),
                   jax.ShapeDtypeStruct((n_tiles * SCONV_HIST_PAD, d_pool), F32)],
        grid=(t // tm,),
        in_specs=[col(0), col(1), col(2), col(3), hist(POOL_HIST_PAD, 0),
                  hist(SCONV_HIST_PAD, 2), hist(SCONV_HIST_PAD, 3)] + const_specs,
        out_specs=[pl.BlockSpec((tm, 2 * d_pool), lambda i: (i, 0)),
                   pl.BlockSpec((SCONV_HIST_PAD, d_pool), lambda i: (jnp.minimum(i, n_tiles - 1), 0))],
        scratch_shapes=scratch(tm),
        compiler_params=_params("arbitrary"), name="seq0_prompt",
    )(z, z, z, z, z, z, z, *consts)

    c0 = n_prompt // CHUNK
    ph = jnp.pad(state_pool, ((0, 0), (POOL_HIST_PAD - state_pool.shape[1], 0), (0, 0)))
    uh = jnp.pad(state_sconv, ((0, 0), (SCONV_HIST_PAD - state_sconv.shape[1], 0), (0, 0)))
    scol = lambda c: pl.BlockSpec((CHUNK, d_pool), lambda i: (c0 + i, c))
    y, utail_s = pl.pallas_call(
        functools.partial(_seq0_sample_body, tm=CHUNK, d_pool=d_pool),
        out_shape=[jax.ShapeDtypeStruct((t, 2 * d_pool), BF16),
                   jax.ShapeDtypeStruct((n_batch * SCONV_HIST_PAD, d_pool), F32)],
        grid=(n_batch,),
        in_specs=[scol(0), scol(1), scol(2), scol(3),
                  pl.BlockSpec((None, POOL_HIST_PAD, d_pool), lambda i: (i, 0, 0)),
                  pl.BlockSpec((None, SCONV_HIST_PAD, d_pool), lambda i: (i, 0, 0))] + const_specs
        + [pl.BlockSpec(memory_space=pl.ANY)],
        out_specs=[pl.BlockSpec((CHUNK, 2 * d_pool), lambda i: (c0 + i, 0)),
                   pl.BlockSpec((SCONV_HIST_PAD, d_pool), lambda i: (i, 0))],
        scratch_shapes=scratch(CHUNK),
        input_output_aliases={9: 0},
        compiler_params=_params("arbitrary"), name="seq0_sample",
    )(z, z, z, z, ph, uh, *consts, y_p)
    return y, utail_p, utail_s


def _seq1_compute(uh, u_ref, wdw_ref, bdw_ref, lng_ref, lnb_ref, o_ref, uext, vbuf, *, tm, width):
    d = u_ref.shape[1]
    uext[pl.ds(0, DWCONV_HIST_PAD), :] = uh
    uext[pl.ds(DWCONV_HIST_PAD, tm), :] = u_ref[...]
    base = DWCONV_HIST_PAD - (width - 1)
    cw = _pick(d, (512, 256, 128))
    rb = _pick(tm, (64,))

    def col_block(cb, carry):
        cols = pl.ds(pl.multiple_of(cb * cw, cw), cw)
        for r in range(tm // rb):
            acc = jnp.broadcast_to(bdw_ref[:, cols], (rb, cw))
            for k in range(width):
                acc = acc + uext[pl.ds(base + k + r * rb, rb), cols] * wdw_ref[pl.ds(k, 1), cols]
            vbuf[pl.ds(r * rb, rb), cols] = acc
        return carry

    lax.fori_loop(0, d // cw, col_block, 0)
    for r in range(tm // rb):
        rows = pl.ds(r * rb, rb)
        v = vbuf[rows, :]
        mu = jnp.mean(v, axis=-1, keepdims=True)
        vc = v - mu
        var = jnp.mean(vc * vc, axis=-1, keepdims=True)
        y = vc * lax.rsqrt(var + LN_EPS) * lng_ref[...] + lnb_ref[...]
        o_ref[rows, :] = (y * jax.nn.sigmoid(y)).astype(o_ref.dtype)


def _seq1_prompt_body(u_ref, uh_ref, wdw_ref, bdw_ref, lng_ref, lnb_ref, o_ref, uext, vbuf,
                      *, tm, width, n_tiles):
    i = pl.program_id(0)

    @pl.when(i < n_tiles)
    def _():
        live = (i > 0).astype(F32)
        _seq1_compute(uh_ref[...] * live, u_ref, wdw_ref, bdw_ref, lng_ref, lnb_ref, o_ref, uext, vbuf,
                      tm=tm, width=width)

    @pl.when(i >= n_tiles)
    def _():
        o_ref[...] = jnp.zeros_like(o_ref)


def _seq1_sample_body(u_ref, uh_ref, wdw_ref, bdw_ref, lng_ref, lnb_ref, prev_ref, o_ref, uext, vbuf,
                      *, tm, width):
    del prev_ref
    _seq1_compute(uh_ref[...], u_ref, wdw_ref, bdw_ref, lng_ref, lnb_ref, o_ref, uext, vbuf,
                  tm=tm, width=width)


def _seq1(u, n_prompt, state_dwconv, w_dw, b_dw, ln_g, ln_b):
    t, d = u.shape
    n_batch = state_dwconv.shape[0]
    width = w_dw.shape[0]
    tm = _pick(n_prompt, (256, 128, 64))
    assert t % tm == 0
    consts = [w_dw, b_dw.reshape(1, d), ln_g.reshape(1, d), ln_b.reshape(1, d)]
    const_specs = [pl.BlockSpec(w_dw.shape, lambda i: (0, 0))] + [
        pl.BlockSpec((1, d), lambda i: (0, 0)) for _ in range(3)]
    scratch = lambda rows: [pltpu.VMEM((DWCONV_HIST_PAD + rows, d), F32), pltpu.VMEM((rows, d), F32)]
    v_p = pl.pallas_call(
        functools.partial(_seq1_prompt_body, tm=tm, width=width, n_tiles=n_prompt // tm),
        out_shape=jax.ShapeDtypeStruct((t, d), BF16),
        grid=(t // tm,),
        in_specs=[pl.BlockSpec((tm, d), lambda i: (i, 0)),
                  pl.BlockSpec((DWCONV_HIST_PAD, d),
                               lambda i: (jnp.maximum(i * (tm // DWCONV_HIST_PAD) - 1, 0), 0))]
        + const_specs,
        out_specs=pl.BlockSpec((tm, d), lambda i: (i, 0)),
        scratch_shapes=scratch(tm),
        compiler_params=_params("arbitrary"), name="seq1_prompt",
    )(u, u, *consts)
    c0 = n_prompt // CHUNK
    uh = jnp.pad(state_dwconv, ((0, 0), (DWCONV_HIST_PAD - state_dwconv.shape[1], 0), (0, 0)))
    return pl.pallas_call(
        functools.partial(_seq1_sample_body, tm=CHUNK, width=width),
        out_shape=jax.ShapeDtypeStruct((t, d), BF16),
        grid=(n_batch,),
        in_specs=[pl.BlockSpec((CHUNK, d), lambda i: (c0 + i, 0)),
                  pl.BlockSpec((None, DWCONV_HIST_PAD, d), lambda i: (i, 0, 0))] + const_specs
        + [pl.BlockSpec(memory_space=pl.ANY)],
        out_specs=pl.BlockSpec((CHUNK, d), lambda i: (c0 + i, 0)),
        scratch_shapes=scratch(CHUNK),
        input_output_aliases={6: 0},
        compiler_params=_params("arbitrary"), name="seq1_sample",
    )(u, uh, *consts, v_p)


def _ffn_body(te_ref, nv_ref, h_ref, wg_ref, wu_ref, wd_ref, o_ref, *, n_chunk):
    del te_ref
    i = pl.program_id(0)
    j = pl.program_id(1)

    @pl.when(i < nv_ref[0])
    def _():
        h = h_ref[...]
        g = jnp.dot(h, wg_ref[...], preferred_element_type=F32)
        u = jnp.dot(h, wu_ref[...], preferred_element_type=F32)
        a = (g * jax.nn.sigmoid(g) * u).astype(BF16)
        d = o_ref.shape[1]
        for c in range(d // n_chunk):
            cols = pl.ds(c * n_chunk, n_chunk)
            part = jnp.dot(a, wd_ref[:, cols], preferred_element_type=F32)

            @pl.when(j == 0)
            def _():
                o_ref[:, cols] = part

            @pl.when(j > 0)
            def _():
                o_ref[:, cols] += part

    @pl.when((i >= nv_ref[0]) & (j == 0))
    def _():
        o_ref[...] = jnp.zeros_like(o_ref)


def _ffn(h, w_gate, w_up, w_down, tile_set, n_valid, *, tm, tf):
    p, d = h.shape
    f = w_gate.shape[2]
    n_tiles, n_f = p // tm, f // tf

    def row_map(i, j, te, nv):
        return (jnp.minimum(i, nv[0] - 1), 0)

    def _ij(i, j, te, nv):
        live = i < nv[0]
        return te[jnp.minimum(i, nv[0] - 1)], jnp.where(live, j, n_f - 1)

    def up_map(i, j, te, nv):
        e, jj = _ij(i, j, te, nv)
        return (e, 0, jj)

    def down_map(i, j, te, nv):
        e, jj = _ij(i, j, te, nv)
        return (e, jj, 0)

    grid_spec = pltpu.PrefetchScalarGridSpec(
        num_scalar_prefetch=2, grid=(n_tiles, n_f),
        in_specs=[pl.BlockSpec((tm, d), row_map),
                  pl.BlockSpec((None, d, tf), up_map),
                  pl.BlockSpec((None, d, tf), up_map),
                  pl.BlockSpec((None, tf, d), down_map)],
        out_specs=pl.BlockSpec((tm, d), lambda i, j, te, nv: (i, 0)))
    return pl.pallas_call(
        functools.partial(_ffn_body, n_chunk=_pick(d, (1024, 512, 256, 128))),
        out_shape=jax.ShapeDtypeStruct((p, d), F32), grid_spec=grid_spec,
        compiler_params=_params("arbitrary", "arbitrary"), name="ffn",
    )(tile_set, n_valid, h, w_gate, w_up, w_down)


def _router_body(h_ref, wr_ref, w_ref, e_ref, *, n_experts):
    logits = jnp.dot(h_ref[...], wr_ref[...], preferred_element_type=F32,
                     precision=lax.Precision.HIGHEST)
    lane = lax.broadcasted_iota(jnp.int32, logits.shape, 1)
    lane_f = lane.astype(F32)
    neg = jnp.float32(-jnp.inf)
    lg = jnp.where(lane < n_experts, logits, neg)
    m1 = jnp.max(lg, axis=-1, keepdims=True)
    i1 = jnp.min(jnp.where(lg == m1, lane_f, float(ROUTE_LANES)), axis=-1, keepdims=True)
    lg2 = jnp.where(lane_f == i1, neg, lg)
    m2 = jnp.max(lg2, axis=-1, keepdims=True)
    i2 = jnp.min(jnp.where(lg2 == m2, lane_f, float(ROUTE_LANES)), axis=-1, keepdims=True)
    ex = jnp.exp(m2 - m1)
    den = 1.0 + ex
    w_ref[...] = jnp.where(lane == 0, 1.0 / den, jnp.where(lane == 1, ex / den, 0.0))
    e_ref[...] = jnp.where(lane == 0, i1, jnp.where(lane == 1, i2, 0.0)).astype(jnp.int32)


def _router(h, w_router):
    t, d = h.shape
    n_experts = w_router.shape[1]
    tm = _pick(t, (256, 128, 64))
    wr = jnp.pad(w_router, ((0, 0), (0, ROUTE_LANES - n_experts)))
    return pl.pallas_call(
        functools.partial(_router_body, n_experts=n_experts),
        out_shape=[jax.ShapeDtypeStruct((t, ROUTE_LANES), F32),
                   jax.ShapeDtypeStruct((t, ROUTE_LANES), jnp.int32)],
        grid=(t // tm,),
        in_specs=[pl.BlockSpec((tm, d), lambda i: (i, 0)),
                  pl.BlockSpec((d, ROUTE_LANES), lambda i: (0, 0))],
        out_specs=[pl.BlockSpec((tm, ROUTE_LANES), lambda i: (i, 0)),
                   pl.BlockSpec((tm, ROUTE_LANES), lambda i: (i, 0))],
        compiler_params=_params("arbitrary"), name="router",
    )(h, wr)


def _row_copy(src_hbm, row, dst, r, sem):
    return pltpu.make_async_copy(src_hbm.at[pl.ds(row, 1), :], dst.at[pl.ds(r, 1), :], sem)


def _gather_body(src_ref, h_hbm, o_ref, buf, sem, *, rows):
    def start(r, c):
        _row_copy(h_hbm, src_ref[0, 0, r], buf, r, sem).start()
        return c

    def wait(r, c):
        _row_copy(h_hbm, 0, buf, r, sem).wait()
        return c

    lax.fori_loop(0, rows, start, 0)
    lax.fori_loop(0, rows, wait, 0)
    o_ref[...] = buf[...].astype(o_ref.dtype)


def _gather_rows(h, src, *, rows):
    d = h.shape[1]
    p = src.shape[0]
    n = p // rows
    return pl.pallas_call(
        functools.partial(_gather_body, rows=rows),
        out_shape=jax.ShapeDtypeStruct((p, d), BF16),
        grid=(n,),
        in_specs=[pl.BlockSpec((1, 1, rows), lambda i: (i, 0, 0), memory_space=pltpu.SMEM),
                  pl.BlockSpec(memory_space=pl.ANY)],
        out_specs=pl.BlockSpec((rows, d), lambda i: (i, 0)),
        scratch_shapes=[pltpu.VMEM((rows, d), F32), pltpu.SemaphoreType.DMA(())],
        compiler_params=_params("arbitrary"), name="gather_rows",
    )(src.reshape(n, 1, rows), h)


def _combine_body(pos_ref, cw_ref, x_ref, gate_ref, g_ref, y_hbm, op_ref, os_ref, buf, sem,
                  *, rows, n_prompt_tiles, n_prompt_chunks):
    i = pl.program_id(0)

    def start(r, c):
        for k in range(TOP_K):
            _row_copy(y_hbm, pos_ref[0, 0, TOP_K * r + k], buf.at[k], r, sem).start()
        return c

    def wait(r, c):
        for k in range(TOP_K):
            _row_copy(y_hbm, 0, buf.at[k], r, sem).wait()
        return c

    lax.fori_loop(0, rows, start, 0)
    lax.fori_loop(0, rows, wait, 0)

    def emit(o_ref):
        for c in range(rows // CHUNK):
            rs = pl.ds(c * CHUNK, CHUNK)
            b = _chunk_batch(i * (rows // CHUNK) + c, n_prompt_chunks)
            cw = cw_ref[rs, :]
            f = cw[:, 0:1] * buf[0, rs, :] + cw[:, 1:2] * buf[1, rs, :]
            xv = x_ref[rs, :] + gate_ref[pl.ds(b, 1), :] * f
            ms = jnp.mean(xv * xv, axis=-1, keepdims=True)
            o_ref[rs, :] = xv * lax.rsqrt(ms + RMS_EPS) * g_ref[...]

    @pl.when(i < n_prompt_tiles)
    def _():
        emit(op_ref)

    @pl.when(i >= n_prompt_tiles)
    def _():
        emit(os_ref)


def _combine(x, y_sorted, pos, cw, mod, gate, g_final, n_prompt):
    t, d = x.shape
    rows = _pick(n_prompt, (256, 128, 64))
    n = t // rows
    npt = n_prompt // rows
    layer, gate_col = gate
    return pl.pallas_call(
        functools.partial(_combine_body, rows=rows, n_prompt_tiles=npt,
                          n_prompt_chunks=n_prompt // CHUNK),
        out_shape=[jax.ShapeDtypeStruct((n_prompt, d), F32),
                   jax.ShapeDtypeStruct((t - n_prompt, d), F32)],
        grid=(n,),
        in_specs=[pl.BlockSpec((1, 1, TOP_K * rows), lambda i: (i, 0, 0), memory_space=pltpu.SMEM),
                  pl.BlockSpec((rows, ROUTE_LANES), lambda i: (i, 0)),
                  pl.BlockSpec((rows, d), lambda i: (i, 0)),
                  pl.BlockSpec((None, MOD_ROWS, d), lambda i: (layer, 0, gate_col)),
                  pl.BlockSpec((1, d), lambda i: (0, 0)),
                  pl.BlockSpec(memory_space=pl.ANY)],
        out_specs=[pl.BlockSpec((rows, d), lambda i: (jnp.minimum(i, npt - 1), 0)),
                   pl.BlockSpec((rows, d), lambda i: (jnp.maximum(i - npt, 0), 0))],
        scratch_shapes=[pltpu.VMEM((TOP_K, rows, d), F32), pltpu.SemaphoreType.DMA(())],
        compiler_params=_params("arbitrary"), name="combine",
    )(pos.reshape(n, 1, TOP_K * rows), cw, x, mod, g_final.reshape(1, d), y_sorted)


def _route_plan(e_idx, n_experts, tm, n_tiles):
    t = e_idx.shape[0]
    flat_e = e_idx.reshape(-1)
    onehot = (flat_e[:, None] == jnp.arange(n_experts)[None, :]).astype(jnp.int32)
    csum = jnp.cumsum(onehot, axis=0)
    rank = jnp.take_along_axis(csum, flat_e[:, None], axis=1)[:, 0] - 1
    counts = csum[-1]
    tiles_per = (counts + tm - 1) // tm
    tile_end = jnp.cumsum(tiles_per)
    row_start = (tile_end - tiles_per) * tm
    pos = row_start[flat_e] + rank
    src = jnp.zeros((n_tiles * tm,), jnp.int32).at[pos].set(jnp.arange(TOP_K * t, dtype=jnp.int32) // TOP_K)
    tile_set = jnp.minimum(jnp.searchsorted(tile_end, jnp.arange(n_tiles), side="right"),
                           n_experts - 1).astype(jnp.int32)
    n_valid = tile_end[-1:].astype(jnp.int32)
    return pos.astype(jnp.int32).reshape(t, TOP_K), src, tile_set, n_valid


def kernel(x_prompt, x_sample, state_pool, state_shortconv, state_dwconv, c_prompt, c_sample, w_ada, b_ada, g_mix, g_ffn, g_final, w_in_ab, w_pool_grp, pool_scale, w_sconv, w_out_ab, w_ffn_gate, w_ffn_up, w_ffn_down, w_pw1, b_pw1, w_dw, b_dw, ln_g, ln_b, w_pw2, b_pw2, w_router, w_exp_gate, w_exp_up, w_exp_down):
    n_pb, seq, d = x_prompt.shape
    n_sb, dec_seq, _ = x_sample.shape
    assert n_pb == 1 and dec_seq == CHUNK and seq % CHUNK == 0
    assert 1 + n_sb <= MOD_ROWS and w_ada.shape[0] == 2
    n_prompt = seq
    t = n_prompt + n_sb * dec_seq
    npc = n_prompt // CHUNK
    d_pool = state_pool.shape[-1]
    n_experts = w_router.shape[-1]

    x = jnp.concatenate([x_prompt.reshape(n_prompt, d), x_sample.reshape(n_sb * dec_seq, d)], axis=0)
    c_all = jnp.concatenate([c_prompt, c_sample, jnp.zeros((MOD_ROWS - 1 - n_sb, d), F32)], axis=0)
    mod = _mod_tables(c_all, w_ada, b_ada)
    bf = lambda w: w.astype(BF16)

    (h,) = _addnorm(x, g_mix[0], npc, mod, shift=(0, 0), scale=(0, 1))
    z = _matmul(h, bf(w_in_ab[0]))
    y_cat, utail_p, utail_s = _seq0(z, n_prompt, state_pool[0], state_shortconv[0],
                                    bf(w_pool_grp[0]), pool_scale[0], w_sconv[0])
    y = _matmul(y_cat, bf(w_out_ab[0]))
    x, h = _addnorm(x, g_ffn[0], npc, mod, y=y, gate=(0, 2), shift=(0, 3), scale=(0, 4))
    tm_d = _pick(t, (768, 512, 256, 128, 64))
    n_td = t // tm_d
    f = _ffn(h, bf(w_ffn_gate), bf(w_ffn_up), bf(w_ffn_down),
             jnp.zeros((n_td,), jnp.int32), jnp.full((1,), n_td, jnp.int32),
             tm=tm_d, tf=_pick(w_ffn_gate.shape[2], (256, 128)))

    x, h = _addnorm(x, g_mix[1], npc, mod, y=f, gate=(0, 5), shift=(1, 0), scale=(1, 1))
    u = _matmul_glu(h, bf(w_pw1[0]), b_pw1[0])
    v = _seq1(u, n_prompt, state_dwconv[0], w_dw[0], b_dw[0], ln_g[0], ln_b[0])
    y = _matmul(v, bf(w_pw2[0]), b_pw2[0])
    x, h32 = _addnorm(x, g_ffn[1], npc, mod, y=y, gate=(1, 2), shift=(1, 3), scale=(1, 4),
                      h_dtypes=(F32,))
    cw, e_idx = _router(h32, w_router[0])
    tm_e = _pick(TOP_K * t, (512, 256, 128, 64))
    n_te = TOP_K * t // tm_e + n_experts
    pos, src, tile_set, n_valid = _route_plan(e_idx[:, :TOP_K], n_experts, tm_e, n_te)
    h_sorted = _gather_rows(h32, src, rows=tm_e)
    y_sorted = _ffn(h_sorted, bf(w_exp_gate[0]), bf(w_exp_up[0]), bf(w_exp_down[0]), tile_set, n_valid,
                    tm=tm_e, tf=_pick(w_exp_gate.shape[3], (512, 256, 128)))
    y_p, y_s = _combine(x, y_sorted, pos, cw, mod, (1, 5), g_final, n_prompt)

    n_tiles0 = utail_p.shape[0] // SCONV_HIST_PAD
    sconv_keep = state_shortconv.shape[2]
    dw_keep = state_dwconv.shape[2]
    zs = z[n_prompt:].reshape(n_sb, dec_seq, -1)
    us = u[n_prompt:].reshape(n_sb, dec_seq, -1)
    new_pool_p = z[n_prompt - POOL_HIST:n_prompt, :d_pool][None, None]
    new_pool_s = zs[:, dec_seq - POOL_HIST:, :d_pool][None]
    new_sconv_p = utail_p[(n_tiles0 - 1) * SCONV_HIST_PAD:][SCONV_HIST_PAD - sconv_keep:][None, None]
    new_sconv_s = utail_s.reshape(n_sb, SCONV_HIST_PAD, -1)[:, SCONV_HIST_PAD - sconv_keep:][None]
    new_dw_p = u[n_prompt - dw_keep:n_prompt][None, None]
    new_dw_s = us[:, dec_seq - dw_keep:][None]
    return (y_p.reshape(1, n_prompt, d), y_s.reshape(n_sb, dec_seq, d),
            new_pool_p, new_sconv_p, new_dw_p, new_pool_s, new_sconv_s, new_dw_s)
```

```python
import functools

import jax
import jax.numpy as jnp
from jax import lax
from jax.experimental import pallas as pl
from jax.experimental.pallas import tpu as pltpu

F32 = jnp.float32
BF16 = jnp.bfloat16

CHUNK = 64
POOL_WINDOWS = (2, 4, 8, 16)
POOL_HIST = max(POOL_WINDOWS) - 1
POOL_HIST_PAD = 16
SCONV_HIST_PAD = 8
DWCONV_HIST_PAD = 32
PAST_LEN = 4096
TOP_K = 2
RMS_EPS = 1e-6
LN_EPS = 1e-5
MOD_ROWS = 16
RING_SLOTS = 2
DMA_UNROLL = 8
SUBLANES = 8
LANES = 128
ROUTE_LANES = LANES
V7X_VMEM_BYTES = 64 * 1024 * 1024
VMEM_LIMIT = V7X_VMEM_BYTES - 6 * 1024 * 1024


def _params(*sem):
    return pltpu.CompilerParams(dimension_semantics=sem, vmem_limit_bytes=VMEM_LIMIT)


def _pick(n, prefs):
    for p in prefs:
        if n % p == 0:
            return p
    return n


def _mod_body(c_ref, w_ref, b_ref, o_ref):
    c = c_ref[...]
    sc = (c * jax.nn.sigmoid(c)).astype(BF16)
    o_ref[...] = jnp.dot(sc, w_ref[...].astype(BF16), preferred_element_type=F32) + b_ref[...]


def _mod_tables(c_all, w_ada, b_ada):
    depth, d, n = w_ada.shape
    tn = _pick(n, (512, 256, 128))
    return pl.pallas_call(
        _mod_body,
        out_shape=jax.ShapeDtypeStruct((depth, MOD_ROWS, n), F32),
        grid=(depth, n // tn),
        in_specs=[
            pl.BlockSpec((MOD_ROWS, d), lambda l, j: (0, 0)),
            pl.BlockSpec((None, d, tn), lambda l, j: (l, 0, j)),
            pl.BlockSpec((None, 1, tn), lambda l, j: (l, 0, j)),
        ],
        out_specs=pl.BlockSpec((None, MOD_ROWS, tn), lambda l, j: (l, 0, j)),
        compiler_params=_params("arbitrary", "arbitrary"),
        name="mod_tables",
    )(c_all, w_ada, b_ada.reshape(depth, 1, n))


def _chunk_batch(chunk_idx, n_prompt_chunks):
    return jnp.maximum(chunk_idx - (n_prompt_chunks - 1), 0)


def _addnorm_body(*refs, tm, n_prompt_chunks, has_add, has_mod):
    refs = list(refs)
    x_ref = refs.pop(0)
    if has_add:
        y_ref = refs.pop(0)
        gate_ref = refs.pop(0)
    g_ref = refs.pop(0)
    if has_mod:
        shift_ref = refs.pop(0)
        scale_ref = refs.pop(0)
    if has_add:
        xo_ref = refs.pop(0)
    h_refs = refs
    i = pl.program_id(0)
    for c in range(tm // CHUNK):
        rows = pl.ds(c * CHUNK, CHUNK)
        b = _chunk_batch(i * (tm // CHUNK) + c, n_prompt_chunks)
        xv = x_ref[rows, :]
        if has_add:
            xv = xv + gate_ref[pl.ds(b, 1), :] * y_ref[rows, :]
            xo_ref[rows, :] = xv
        ms = jnp.mean(xv * xv, axis=-1, keepdims=True)
        hv = xv * lax.rsqrt(ms + RMS_EPS) * g_ref[...]
        if has_mod:
            hv = hv * (1.0 + scale_ref[pl.ds(b, 1), :]) + shift_ref[pl.ds(b, 1), :]
        for h_ref in h_refs:
            h_ref[rows, :] = hv.astype(h_ref.dtype)


def _addnorm(x, g, n_prompt_chunks, mod, *, y=None, gate=None, shift=None, scale=None,
             h_dtypes=(BF16,)):
    t, d = x.shape
    tm = _pick(t, (256, 128, 64))
    has_add = y is not None
    has_mod = shift is not None
    row = pl.BlockSpec((tm, d), lambda i: (i, 0))

    def tab(where):
        layer, col = where
        return pl.BlockSpec((None, MOD_ROWS, d), lambda i: (layer, 0, col))

    args, specs = [x], [row]
    if has_add:
        args += [y, mod]
        specs += [row, tab(gate)]
    args.append(g.reshape(1, d))
    specs.append(pl.BlockSpec((1, d), lambda i: (0, 0)))
    if has_mod:
        args += [mod, mod]
        specs += [tab(shift), tab(scale)]
    out_shape, out_specs = [], []
    if has_add:
        out_shape.append(jax.ShapeDtypeStruct((t, d), F32))
        out_specs.append(row)
    for dt in h_dtypes:
        out_shape.append(jax.ShapeDtypeStruct((t, d), dt))
        out_specs.append(row)
    body = functools.partial(_addnorm_body, tm=tm, n_prompt_chunks=n_prompt_chunks,
                             has_add=has_add, has_mod=has_mod)
    return pl.pallas_call(
        body, out_shape=out_shape, grid=(t // tm,), in_specs=specs, out_specs=out_specs,
        compiler_params=_params("arbitrary"), name="addnorm",
    )(*args)


def _mm_body(a_ref, b_ref, *refs, has_bias):
    o_ref = refs[-1]
    acc = jnp.dot(a_ref[...], b_ref[...], preferred_element_type=F32)
    if has_bias:
        acc = acc + refs[0][...]
    o_ref[...] = acc


def _matmul(a, b, bias=None):
    t, k = a.shape
    n = b.shape[1]
    tm = _pick(t, (768, 512, 256, 128, 64))
    tn = _pick(n, (1024, 512, 256, 128))
    args = [a, b]
    specs = [pl.BlockSpec((tm, k), lambda i, j: (i, 0)), pl.BlockSpec((k, tn), lambda i, j: (0, j))]
    if bias is not None:
        args.append(bias.reshape(1, n))
        specs.append(pl.BlockSpec((1, tn), lambda i, j: (0, j)))
    return pl.pallas_call(
        functools.partial(_mm_body, has_bias=bias is not None),
        out_shape=jax.ShapeDtypeStruct((t, n), F32),
        grid=(t // tm, n // tn), in_specs=specs,
        out_specs=pl.BlockSpec((tm, tn), lambda i, j: (i, j)),
        compiler_params=_params("arbitrary", "arbitrary"), name="matmul",
    )(*args)


def _glu_body(a_ref, ba_ref, bg_ref, bias_a_ref, bias_g_ref, o_ref):
    a = a_ref[...]
    za = jnp.dot(a, ba_ref[...], preferred_element_type=F32) + bias_a_ref[...]
    zg = jnp.dot(a, bg_ref[...], preferred_element_type=F32) + bias_g_ref[...]
    o_ref[...] = za * jax.nn.sigmoid(zg)


def _matmul_glu(a, b, bias):
    t, k = a.shape
    n = b.shape[1] // 2
    tm = _pick(t, (768, 512, 256, 128, 64))
    tn = _pick(n, (512, 256, 128))
    nb = n // tn
    bias = bias.reshape(1, 2 * n)
    return pl.pallas_call(
        _glu_body,
        out_shape=jax.ShapeDtypeStruct((t, n), F32),
        grid=(t // tm, nb),
        in_specs=[
            pl.BlockSpec((tm, k), lambda i, j: (i, 0)),
            pl.BlockSpec((k, tn), lambda i, j: (0, j)),
            pl.BlockSpec((k, tn), lambda i, j: (0, j + nb)),
            pl.BlockSpec((1, tn), lambda i, j: (0, j)),
            pl.BlockSpec((1, tn), lambda i, j: (0, j + nb)),
        ],
        out_specs=pl.BlockSpec((tm, tn), lambda i, j: (i, j)),
        compiler_params=_params("arbitrary", "arbitrary"), name="matmul_glu",
    )(a, b, b, bias, bias)


def _seq0_compute(pos_base, ph, uh, p_ref, bg_ref, cg_ref, v_ref, wgrp_ref, scale_ref, wsc_ref,
                  o_ref, utail_ref, pext, uext, *, tm, d_pool):
    n_groups = len(POOL_WINDOWS)
    gw = d_pool // n_groups
    pext[pl.ds(0, POOL_HIST_PAD), :] = ph
    pext[pl.ds(POOL_HIST_PAD, tm), :] = p_ref[...]
    u = cg_ref[...] * v_ref[...]
    uext[pl.ds(0, SCONV_HIST_PAD), :] = uh
    uext[pl.ds(SCONV_HIST_PAD, tm), :] = u
    utail_ref[...] = u[tm - SCONV_HIST_PAD:, :]

    pos = pos_base + lax.broadcasted_iota(jnp.int32, (tm, 1), 0)
    for gi, w in enumerate(POOL_WINDOWS):
        cols = pl.ds(gi * gw, gw)
        cur = pext[pl.ds(POOL_HIST_PAD, tm), cols]
        s = cur
        for k in range(1, w):
            s = s + pext[pl.ds(POOL_HIST_PAD - k, tm), cols]
        cnt = jnp.minimum(w, pos + 1).astype(F32)
        dlt = (s / cnt - cur).astype(BF16)
        ya = jnp.dot(dlt, wgrp_ref[gi], preferred_element_type=F32) * scale_ref[:, cols]
        o_ref[:, cols] = ya.astype(o_ref.dtype)

    conv = (uext[pl.ds(SCONV_HIST_PAD - 2, tm), :] * wsc_ref[pl.ds(0, 1), :]
            + uext[pl.ds(SCONV_HIST_PAD - 1, tm), :] * wsc_ref[pl.ds(1, 1), :]
            + uext[pl.ds(SCONV_HIST_PAD, tm), :] * wsc_ref[pl.ds(2, 1), :])
    o_ref[:, pl.ds(d_pool, d_pool)] = (bg_ref[...] * conv).astype(o_ref.dtype)


def _seq0_prompt_body(p_ref, bg_ref, cg_ref, v_ref, ph_ref, cgh_ref, vh_ref, wgrp_ref, scale_ref,
                      wsc_ref, o_ref, utail_ref, pext, uext, *, tm, d_pool, n_tiles):
    i = pl.program_id(0)

    @pl.when(i < n_tiles)
    def _():
        live = (i > 0).astype(F32)
        ph = ph_ref[...] * live
        uh = cgh_ref[...] * vh_ref[...] * live
        _seq0_compute(i * tm, ph, uh, p_ref, bg_ref, cg_ref, v_ref, wgrp_ref, scale_ref, wsc_ref,
                      o_ref, utail_ref, pext, uext, tm=tm, d_pool=d_pool)

    @pl.when(i >= n_tiles)
    def _():
        o_ref[...] = jnp.zeros_like(o_ref)


def _seq0_sample_body(p_ref, bg_ref, cg_ref, v_ref, ph_ref, uh_ref, wgrp_ref, scale_ref, wsc_ref,
                      prev_ref, o_ref, utail_ref, pext, uext, *, tm, d_pool):
    del prev_ref
    _seq0_compute(PAST_LEN, ph_ref[...], uh_ref[...], p_ref, bg_ref, cg_ref, v_ref, wgrp_ref,
                  scale_ref, wsc_ref, o_ref, utail_ref, pext, uext, tm=tm, d_pool=d_pool)


def _seq0(z, n_prompt, state_pool, state_sconv, w_grp, pool_scale, w_sconv):
    t = z.shape[0]
    n_batch, _, d_pool = state_pool.shape
    tm = _pick(n_prompt, (256, 128, 64))
    assert t % tm == 0
    n_tiles = n_prompt // tm
    scratch = lambda rows: [pltpu.VMEM((POOL_HIST_PAD + rows, d_pool), F32),
                            pltpu.VMEM((SCONV_HIST_PAD + rows, d_pool), F32)]
    consts = [w_grp, pool_scale.reshape(1, d_pool), w_sconv]
    const_specs = [pl.BlockSpec(w_grp.shape, lambda i: (0, 0, 0)),
                   pl.BlockSpec((1, d_pool), lambda i: (0, 0)),
                   pl.BlockSpec(w_sconv.shape, lambda i: (0, 0))]
    col = lambda c: pl.BlockSpec((tm, d_pool), lambda i: (i, c))
    hist = lambda rows, c: pl.BlockSpec(
        (rows, d_pool), lambda i: (jnp.maximum(i * (tm // rows) - 1, 0), c))
    y_p, utail_p = pl.pallas_call(
        functools.partial(_seq0_prompt_body, tm=tm, d_pool=d_pool, n_tiles=n_tiles),
        out_shape=[jax.ShapeDtypeStruct((t, 2 * d_pool), BF16),
                   jax.ShapeDtypeStruct((n_tiles * SCONV_HIST_PAD, d_pool), F32)],
        grid=(t // tm,),
        in_specs=[col(0), col(1), col(2), col(3), hist(POOL_HIST_PAD, 0),
                  hist(SCONV_HIST_PAD, 2), hist(SCONV_HIST_PAD, 3)] + const_specs,
        out_specs=[pl.BlockSpec((tm, 2 * d_pool), lambda i: (i, 0)),
                   pl.BlockSpec((SCONV_HIST_PAD, d_pool), lambda i: (jnp.minimum(i, n_tiles - 1), 0))],
        scratch_shapes=scratch(tm),
        compiler_params=_params("arbitrary"), name="seq0_prompt",
    )(z, z, z, z, z, z, z, *consts)

    c0 = n_prompt // CHUNK
    ph = jnp.pad(state_pool, ((0, 0), (POOL_HIST_PAD - state_pool.shape[1], 0), (0, 0)))
    uh = jnp.pad(state_sconv, ((0, 0), (SCONV_HIST_PAD - state_sconv.shape[1], 0), (0, 0)))
    scol = lambda c: pl.BlockSpec((CHUNK, d_pool), lambda i: (c0 + i, c))
    y, utail_s = pl.pallas_call(
        functools.partial(_seq0_sample_body, tm=CHUNK, d_pool=d_pool),
        out_shape=[jax.ShapeDtypeStruct((t, 2 * d_pool), BF16),
                   jax.ShapeDtypeStruct((n_batch * SCONV_HIST_PAD, d_pool), F32)],
        grid=(n_batch,),
        in_specs=[scol(0), scol(1), scol(2), scol(3),
                  pl.BlockSpec((None, POOL_HIST_PAD, d_pool), lambda i: (i, 0, 0)),
                  pl.BlockSpec((None, SCONV_HIST_PAD, d_pool), lambda i: (i, 0, 0))] + const_specs
        + [pl.BlockSpec(memory_space=pl.ANY)],
        out_specs=[pl.BlockSpec((CHUNK, 2 * d_pool), lambda i: (c0 + i, 0)),
                   pl.BlockSpec((SCONV_HIST_PAD, d_pool), lambda i: (i, 0))],
        scratch_shapes=scratch(CHUNK),
        input_output_aliases={9: 0},
        compiler_params=_params("arbitrary"), name="seq0_sample",
    )(z, z, z, z, ph, uh, *consts, y_p)
    return y, utail_p, utail_s


def _seq1_compute(uh, u_ref, wdw_ref, bdw_ref, lng_ref, lnb_ref, o_ref, uext, vbuf, *, tm, width):
    d = u_ref.shape[1]
    uext[pl.ds(0, DWCONV_HIST_PAD), :] = uh
    uext[pl.ds(DWCONV_HIST_PAD, tm), :] = u_ref[...]
    base = DWCONV_HIST_PAD - (width - 1)
    assert base >= 0 and width - 1 + base <= DWCONV_HIST_PAD
    cw = LANES
    rb = _pick(tm, (128, 64))

    def col_block(cb, carry):
        cols = pl.ds(pl.multiple_of(cb * cw, cw), cw)
        for r in range(tm // rb):
            acc = jnp.broadcast_to(bdw_ref[:, cols], (rb, cw))
            for res in range(SUBLANES):
                taps = [k for k in range(width) if (base + k) % SUBLANES == res]
                n = rb if res == 0 else rb + SUBLANES
                part = None
                for k in taps:
                    term = uext[pl.ds(r * rb + base + k - res, n), cols] * wdw_ref[pl.ds(k, 1), cols]
                    part = term if part is None else part + term
                if part is not None:
                    acc = acc + part[res:res + rb]
            vbuf[pl.ds(r * rb, rb), cols] = acc
        return carry

    lax.fori_loop(0, d // cw, col_block, 0)
    rb = CHUNK
    for r in range(tm // rb):
        rows = pl.ds(r * rb, rb)
        v = vbuf[rows, :]
        mu = jnp.mean(v, axis=-1, keepdims=True)
        vc = v - mu
        var = jnp.mean(vc * vc, axis=-1, keepdims=True)
        y = vc * lax.rsqrt(var + LN_EPS) * lng_ref[...] + lnb_ref[...]
        o_ref[rows, :] = (y * jax.nn.sigmoid(y)).astype(o_ref.dtype)


def _seq1_prompt_body(u_ref, uh_ref, wdw_ref, bdw_ref, lng_ref, lnb_ref, o_ref, uext, vbuf,
                      *, tm, width, n_tiles):
    i = pl.program_id(0)

    @pl.when(i < n_tiles)
    def _():
        live = (i > 0).astype(F32)
        _seq1_compute(uh_ref[...] * live, u_ref, wdw_ref, bdw_ref, lng_ref, lnb_ref, o_ref, uext, vbuf,
                      tm=tm, width=width)

    @pl.when(i >= n_tiles)
    def _():
        o_ref[...] = jnp.zeros_like(o_ref)


def _seq1_sample_body(u_ref, uh_ref, wdw_ref, bdw_ref, lng_ref, lnb_ref, prev_ref, o_ref, uext, vbuf,
                      *, tm, width):
    del prev_ref
    _seq1_compute(uh_ref[...], u_ref, wdw_ref, bdw_ref, lng_ref, lnb_ref, o_ref, uext, vbuf,
                  tm=tm, width=width)


def _seq1(u, n_prompt, state_dwconv, w_dw, b_dw, ln_g, ln_b):
    t, d = u.shape
    n_batch = state_dwconv.shape[0]
    width = w_dw.shape[0]
    tm = _pick(n_prompt, (256, 128, 64))
    assert t % tm == 0
    consts = [w_dw, b_dw.reshape(1, d), ln_g.reshape(1, d), ln_b.reshape(1, d)]
    const_specs = [pl.BlockSpec(w_dw.shape, lambda i: (0, 0))] + [
        pl.BlockSpec((1, d), lambda i: (0, 0)) for _ in range(3)]
    scratch = lambda rows: [pltpu.VMEM((DWCONV_HIST_PAD + rows, d), F32), pltpu.VMEM((rows, d), F32)]
    v_p = pl.pallas_call(
        functools.partial(_seq1_prompt_body, tm=tm, width=width, n_tiles=n_prompt // tm),
        out_shape=jax.ShapeDtypeStruct((t, d), BF16),
        grid=(t // tm,),
        in_specs=[pl.BlockSpec((tm, d), lambda i: (i, 0)),
                  pl.BlockSpec((DWCONV_HIST_PAD, d),
                               lambda i: (jnp.maximum(i * (tm // DWCONV_HIST_PAD) - 1, 0), 0))]
        + const_specs,
        out_specs=pl.BlockSpec((tm, d), lambda i: (i, 0)),
        scratch_shapes=scratch(tm),
        compiler_params=_params("arbitrary"), name="seq1_prompt",
    )(u, u, *consts)
    c0 = n_prompt // CHUNK
    uh = jnp.pad(state_dwconv, ((0, 0), (DWCONV_HIST_PAD - state_dwconv.shape[1], 0), (0, 0)))
    return pl.pallas_call(
        functools.partial(_seq1_sample_body, tm=CHUNK, width=width),
        out_shape=jax.ShapeDtypeStruct((t, d), BF16),
        grid=(n_batch,),
        in_specs=[pl.BlockSpec((CHUNK, d), lambda i: (c0 + i, 0)),
                  pl.BlockSpec((None, DWCONV_HIST_PAD, d), lambda i: (i, 0, 0))] + const_specs
        + [pl.BlockSpec(memory_space=pl.ANY)],
        out_specs=pl.BlockSpec((CHUNK, d), lambda i: (c0 + i, 0)),
        scratch_shapes=scratch(CHUNK),
        input_output_aliases={6: 0},
        compiler_params=_params("arbitrary"), name="seq1_sample",
    )(u, uh, *consts, v_p)


def _ffn_body(te_ref, nv_ref, h_ref, wg_ref, wu_ref, wd_ref, o_ref, *, n_chunk):
    del te_ref
    i = pl.program_id(0)
    j = pl.program_id(1)

    @pl.when(j == 0)
    def _():
        o_ref[...] = jnp.zeros_like(o_ref)

    @pl.when(i < nv_ref[0])
    def _():
        h = h_ref[...]
        g = jnp.dot(h, wg_ref[...], preferred_element_type=F32)
        u = jnp.dot(h, wu_ref[...], preferred_element_type=F32)
        a = (g * jax.nn.sigmoid(g) * u).astype(BF16)
        d = o_ref.shape[1]
        for c in range(d // n_chunk):
            cols = pl.ds(c * n_chunk, n_chunk)
            o_ref[:, cols] += jnp.dot(a, wd_ref[:, cols], preferred_element_type=F32)


def _ffn(h, w_gate, w_up, w_down, tile_set, n_valid, *, tm, tf):
    p, d = h.shape
    f = w_gate.shape[2]
    n_tiles, n_f = p // tm, f // tf

    def row_map(i, j, te, nv):
        return (jnp.minimum(i, nv[0] - 1), 0)

    def _ij(i, j, te, nv):
        live = i < nv[0]
        return te[jnp.minimum(i, nv[0] - 1)], jnp.where(live, j, n_f - 1)

    def up_map(i, j, te, nv):
        e, jj = _ij(i, j, te, nv)
        return (e, 0, jj)

    def down_map(i, j, te, nv):
        e, jj = _ij(i, j, te, nv)
        return (e, jj, 0)

    grid_spec = pltpu.PrefetchScalarGridSpec(
        num_scalar_prefetch=2, grid=(n_tiles, n_f),
        in_specs=[pl.BlockSpec((tm, d), row_map),
                  pl.BlockSpec((None, d, tf), up_map),
                  pl.BlockSpec((None, d, tf), up_map),
                  pl.BlockSpec((None, tf, d), down_map)],
        out_specs=pl.BlockSpec((tm, d), lambda i, j, te, nv: (i, 0)))
    return pl.pallas_call(
        functools.partial(_ffn_body, n_chunk=_pick(d, (1024, 512, 256, 128))),
        out_shape=jax.ShapeDtypeStruct((p, d), F32), grid_spec=grid_spec,
        compiler_params=_params("arbitrary", "arbitrary"), name="ffn",
    )(tile_set, n_valid, h, w_gate, w_up, w_down)


def _router_body(h_ref, wr_ref, w_ref, e_ref, *, n_experts):
    logits = jnp.dot(h_ref[...], wr_ref[...], preferred_element_type=F32,
                     precision=lax.Precision.HIGHEST)
    lane = lax.broadcasted_iota(jnp.int32, logits.shape, 1)
    lane_f = lane.astype(F32)
    neg = jnp.float32(-jnp.inf)
    lg = jnp.where(lane < n_experts, logits, neg)
    m1 = jnp.max(lg, axis=-1, keepdims=True)
    i1 = jnp.min(jnp.where(lg == m1, lane_f, float(ROUTE_LANES)), axis=-1, keepdims=True)
    lg2 = jnp.where(lane_f == i1, neg, lg)
    m2 = jnp.max(lg2, axis=-1, keepdims=True)
    i2 = jnp.min(jnp.where(lg2 == m2, lane_f, float(ROUTE_LANES)), axis=-1, keepdims=True)
    ex = jnp.exp(m2 - m1)
    den = 1.0 + ex
    w_ref[...] = jnp.where(lane == 0, 1.0 / den, jnp.where(lane == 1, ex / den, 0.0))
    e_ref[...] = jnp.where(lane == 0, i1, jnp.where(lane == 1, i2, 0.0)).astype(jnp.int32)


def _router(h, w_router):
    t, d = h.shape
    n_experts = w_router.shape[1]
    tm = _pick(t, (256, 128, 64))
    wr = jnp.pad(w_router, ((0, 0), (0, ROUTE_LANES - n_experts)))
    return pl.pallas_call(
        functools.partial(_router_body, n_experts=n_experts),
        out_shape=[jax.ShapeDtypeStruct((t, ROUTE_LANES), F32),
                   jax.ShapeDtypeStruct((t, ROUTE_LANES), jnp.int32)],
        grid=(t // tm,),
        in_specs=[pl.BlockSpec((tm, d), lambda i: (i, 0)),
                  pl.BlockSpec((d, ROUTE_LANES), lambda i: (0, 0))],
        out_specs=[pl.BlockSpec((tm, ROUTE_LANES), lambda i: (i, 0)),
                   pl.BlockSpec((tm, ROUTE_LANES), lambda i: (i, 0))],
        compiler_params=_params("arbitrary"), name="router",
    )(h, wr)


def _row_copy(src_hbm, row, dst, r, sem):
    return pltpu.make_async_copy(src_hbm.at[pl.ds(row, 1), :], dst.at[pl.ds(r, 1), :], sem)


def _ring_step(n_steps, start_rows, wait_rows):
    i = pl.program_id(0)
    slot = lax.rem(i, RING_SLOTS)

    @pl.when(i == 0)
    def _():
        start_rows(False, 0)

    @pl.when(i + 1 < n_steps)
    def _():
        start_rows(True, 1 - slot)

    wait_rows(slot)
    return slot


def _gather_body(src_ref, src_next_ref, h_hbm, o_ref, buf, sem, *, rows, n_steps):
    def start_rows(is_next, slot):
        idx = src_next_ref if is_next else src_ref

        def start(r, c):
            _row_copy(h_hbm, idx[0, 0, r], buf.at[slot], r, sem.at[slot]).start()
            return c

        lax.fori_loop(0, rows, start, 0, unroll=DMA_UNROLL)

    def wait_rows(slot):
        def wait(r, c):
            _row_copy(h_hbm, 0, buf.at[slot], r, sem.at[slot]).wait()
            return c

        lax.fori_loop(0, rows, wait, 0, unroll=DMA_UNROLL)

    slot = _ring_step(n_steps, start_rows, wait_rows)
    o_ref[...] = buf[slot].astype(o_ref.dtype)


def _gather_rows(h, src, *, rows):
    d = h.shape[1]
    p = src.shape[0]
    n = p // rows
    src = src.reshape(n, 1, rows)
    idx_spec = lambda step: pl.BlockSpec((1, 1, rows), lambda i: (jnp.minimum(i + step, n - 1), 0, 0),
                                         memory_space=pltpu.SMEM)
    return pl.pallas_call(
        functools.partial(_gather_body, rows=rows, n_steps=n),
        out_shape=jax.ShapeDtypeStruct((p, d), BF16),
        grid=(n,),
        in_specs=[idx_spec(0), idx_spec(1), pl.BlockSpec(memory_space=pl.ANY)],
        out_specs=pl.BlockSpec((rows, d), lambda i: (i, 0)),
        scratch_shapes=[pltpu.VMEM((RING_SLOTS, rows, d), F32), pltpu.SemaphoreType.DMA((RING_SLOTS,))],
        compiler_params=_params("arbitrary"), name="gather_rows",
    )(src, src, h)


def _combine_body(pos_ref, pos_next_ref, cw_ref, x_ref, gate_ref, g_ref, y_hbm, op_ref, os_ref, buf, sem,
                  *, rows, n_steps, n_prompt_tiles, n_prompt_chunks):
    i = pl.program_id(0)

    def start_rows(is_next, slot):
        idx = pos_next_ref if is_next else pos_ref

        def start(r, c):
            for k in range(TOP_K):
                _row_copy(y_hbm, idx[0, 0, TOP_K * r + k], buf.at[slot, k], r, sem.at[slot]).start()
            return c

        lax.fori_loop(0, rows, start, 0, unroll=DMA_UNROLL)

    def wait_rows(slot):
        def wait(r, c):
            for k in range(TOP_K):
                _row_copy(y_hbm, 0, buf.at[slot, k], r, sem.at[slot]).wait()
            return c

        lax.fori_loop(0, rows, wait, 0, unroll=DMA_UNROLL)

    slot = _ring_step(n_steps, start_rows, wait_rows)

    def emit(o_ref):
        for c in range(rows // CHUNK):
            rs = pl.ds(c * CHUNK, CHUNK)
            b = _chunk_batch(i * (rows // CHUNK) + c, n_prompt_chunks)
            cw = cw_ref[rs, :]
            f = cw[:, 0:1] * buf[slot, 0, rs, :] + cw[:, 1:2] * buf[slot, 1, rs, :]
            xv = x_ref[rs, :] + gate_ref[pl.ds(b, 1), :] * f
            ms = jnp.mean(xv * xv, axis=-1, keepdims=True)
            o_ref[rs, :] = xv * lax.rsqrt(ms + RMS_EPS) * g_ref[...]

    @pl.when(i < n_prompt_tiles)
    def _():
        emit(op_ref)

    @pl.when(i >= n_prompt_tiles)
    def _():
        emit(os_ref)


def _combine(x, y_sorted, pos, cw, mod, gate, g_final, n_prompt):
    t, d = x.shape
    rows = _pick(n_prompt, (256, 128, 64))
    n = t // rows
    npt = n_prompt // rows
    layer, gate_col = gate
    pos = pos.reshape(n, 1, TOP_K * rows)
    idx_spec = lambda step: pl.BlockSpec((1, 1, TOP_K * rows),
                                         lambda i: (jnp.minimum(i + step, n - 1), 0, 0),
                                         memory_space=pltpu.SMEM)
    return pl.pallas_call(
        functools.partial(_combine_body, rows=rows, n_steps=n, n_prompt_tiles=npt,
                          n_prompt_chunks=n_prompt // CHUNK),
        out_shape=[jax.ShapeDtypeStruct((n_prompt, d), F32),
                   jax.ShapeDtypeStruct((t - n_prompt, d), F32)],
        grid=(n,),
        in_specs=[idx_spec(0), idx_spec(1),
                  pl.BlockSpec((rows, ROUTE_LANES), lambda i: (i, 0)),
                  pl.BlockSpec((rows, d), lambda i: (i, 0)),
                  pl.BlockSpec((None, MOD_ROWS, d), lambda i: (layer, 0, gate_col)),
                  pl.BlockSpec((1, d), lambda i: (0, 0)),
                  pl.BlockSpec(memory_space=pl.ANY)],
        out_specs=[pl.BlockSpec((rows, d), lambda i: (jnp.minimum(i, npt - 1), 0)),
                   pl.BlockSpec((rows, d), lambda i: (jnp.maximum(i - npt, 0), 0))],
        scratch_shapes=[pltpu.VMEM((RING_SLOTS, TOP_K, rows, d), F32),
                        pltpu.SemaphoreType.DMA((RING_SLOTS,))],
        compiler_params=_params("arbitrary"), name="combine",
    )(pos, pos, cw, x, mod, g_final.reshape(1, d), y_sorted)


def _route_plan(e_idx, n_experts, tm, n_tiles):
    t = e_idx.shape[0]
    flat_e = e_idx.reshape(-1)
    onehot = (flat_e[:, None] == jnp.arange(n_experts)[None, :]).astype(jnp.int32)
    csum = jnp.cumsum(onehot, axis=0)
    rank = jnp.take_along_axis(csum, flat_e[:, None], axis=1)[:, 0] - 1
    counts = csum[-1]
    tiles_per = (counts + tm - 1) // tm
    tile_end = jnp.cumsum(tiles_per)
    row_start = (tile_end - tiles_per) * tm
    pos = row_start[flat_e] + rank
    src = jnp.zeros((n_tiles * tm,), jnp.int32).at[pos].set(jnp.arange(TOP_K * t, dtype=jnp.int32) // TOP_K)
    before = (tile_end[None, :] <= jnp.arange(n_tiles)[:, None]).astype(jnp.int32)
    tile_set = jnp.minimum(jnp.sum(before, axis=1), n_experts - 1).astype(jnp.int32)
    n_valid = tile_end[-1:].astype(jnp.int32)
    return pos.astype(jnp.int32).reshape(t, TOP_K), src, tile_set, n_valid


def kernel(x_prompt, x_sample, state_pool, state_shortconv, state_dwconv, c_prompt, c_sample, w_ada, b_ada, g_mix, g_ffn, g_final, w_in_ab, w_pool_grp, pool_scale, w_sconv, w_out_ab, w_ffn_gate, w_ffn_up, w_ffn_down, w_pw1, b_pw1, w_dw, b_dw, ln_g, ln_b, w_pw2, b_pw2, w_router, w_exp_gate, w_exp_up, w_exp_down):
    n_pb, seq, d = x_prompt.shape
    n_sb, dec_seq, _ = x_sample.shape
    assert n_pb == 1 and dec_seq == CHUNK and seq % CHUNK == 0
    assert 1 + n_sb <= MOD_ROWS and w_ada.shape[0] == 2
    n_prompt = seq
    t = n_prompt + n_sb * dec_seq
    npc = n_prompt // CHUNK
    d_pool = state_pool.shape[-1]
    n_experts = w_router.shape[-1]

    x = jnp.concatenate([x_prompt.reshape(n_prompt, d), x_sample.reshape(n_sb * dec_seq, d)], axis=0)
    c_all = jnp.concatenate([c_prompt, c_sample, jnp.zeros((MOD_ROWS - 1 - n_sb, d), F32)], axis=0)
    mod = _mod_tables(c_all, w_ada, b_ada)
    bf = lambda w: w.astype(BF16)

    (h,) = _addnorm(x, g_mix[0], npc, mod, shift=(0, 0), scale=(0, 1))
    z = _matmul(h, bf(w_in_ab[0]))
    y_cat, utail_p, utail_s = _seq0(z, n_prompt, state_pool[0], state_shortconv[0],
                                    bf(w_pool_grp[0]), pool_scale[0], w_sconv[0])
    y = _matmul(y_cat, bf(w_out_ab[0]))
    x, h = _addnorm(x, g_ffn[0], npc, mod, y=y, gate=(0, 2), shift=(0, 3), scale=(0, 4))
    tm_d = _pick(t, (768, 512, 256, 128, 64))
    n_td = t // tm_d
    f = _ffn(h, bf(w_ffn_gate), bf(w_ffn_up), bf(w_ffn_down),
             jnp.zeros((n_td,), jnp.int32), jnp.full((1,), n_td, jnp.int32),
             tm=tm_d, tf=_pick(w_ffn_gate.shape[2], (256, 128)))

    x, h = _addnorm(x, g_mix[1], npc, mod, y=f, gate=(0, 5), shift=(1, 0), scale=(1, 1))
    u = _matmul_glu(h, bf(w_pw1[0]), b_pw1[0])
    v = _seq1(u, n_prompt, state_dwconv[0], w_dw[0], b_dw[0], ln_g[0], ln_b[0])
    y = _matmul(v, bf(w_pw2[0]), b_pw2[0])
    x, h32 = _addnorm(x, g_ffn[1], npc, mod, y=y, gate=(1, 2), shift=(1, 3), scale=(1, 4),
                      h_dtypes=(F32,))
    cw, e_idx = _router(h32, w_router[0])
    tm_e = _pick(TOP_K * t, (512, 256, 128, 64))
    n_te = TOP_K * t // tm_e + n_experts
    pos, src, tile_set, n_valid = _route_plan(e_idx[:, :TOP_K], n_experts, tm_e, n_te)
    h_sorted = _gather_rows(h32, src, rows=tm_e)
    y_sorted = _ffn(h_sorted, bf(w_exp_gate[0]), bf(w_exp_up[0]), bf(w_exp_down[0]), tile_set, n_valid,
                    tm=tm_e, tf=_pick(w_exp_gate.shape[3], (512, 256, 128)))
    y_p, y_s = _combine(x, y_sorted, pos, cw, mod, (1, 5), g_final, n_prompt)

    n_tiles0 = utail_p.shape[0] // SCONV_HIST_PAD
    sconv_keep = state_shortconv.shape[2]
    dw_keep = state_dwconv.shape[2]
    zs = z[n_prompt:].reshape(n_sb, dec_seq, -1)
    us = u[n_prompt:].reshape(n_sb, dec_seq, -1)
    new_pool_p = z[n_prompt - POOL_HIST:n_prompt, :d_pool][None, None]
    new_pool_s = zs[:, dec_seq - POOL_HIST:, :d_pool][None]
    new_sconv_p = utail_p[(n_tiles0 - 1) * SCONV_HIST_PAD:][SCONV_HIST_PAD - sconv_keep:][None, None]
    new_sconv_s = utail_s.reshape(n_sb, SCONV_HIST_PAD, -1)[:, SCONV_HIST_PAD - sconv_keep:][None]
    new_dw_p = u[n_prompt - dw_keep:n_prompt][None, None]
    new_dw_s = us[:, dec_seq - dw_keep:][None]
    return (y_p.reshape(1, n_prompt, d), y_s.reshape(n_sb, dec_seq, d),
            new_pool_p, new_sconv_p, new_dw_p, new_pool_s, new_sconv_s, new_dw_s)
```

```python
import functools

import jax
import jax.numpy as jnp
from jax import lax
from jax.experimental import pallas as pl
from jax.experimental.pallas import tpu as pltpu

F32 = jnp.float32
BF16 = jnp.bfloat16

CHUNK = 64
POOL_WINDOWS = (2, 4, 8, 16)
POOL_HIST = max(POOL_WINDOWS) - 1
POOL_HIST_PAD = 16
SCONV_HIST_PAD = 8
DWCONV_HIST_PAD = 32
PAST_LEN = 4096
TOP_K = 2
RMS_EPS = 1e-6
LN_EPS = 1e-5
MOD_ROWS = 16
RING_SLOTS = 2
DMA_UNROLL = 8
SUBLANES = 8
LANES = 128
ROUTE_LANES = LANES
V7X_VMEM_BYTES = 64 * 1024 * 1024
VMEM_LIMIT = V7X_VMEM_BYTES - 6 * 1024 * 1024


def _params(*sem):
    return pltpu.CompilerParams(dimension_semantics=sem, vmem_limit_bytes=VMEM_LIMIT)


def _pick(n, prefs):
    for p in prefs:
        if n % p == 0:
            return p
    return n


def _mod_body(c_ref, w_ref, b_ref, o_ref):
    c = c_ref[...]
    sc = (c * jax.nn.sigmoid(c)).astype(BF16)
    o_ref[...] = jnp.dot(sc, w_ref[...].astype(BF16), preferred_element_type=F32) + b_ref[...]


def _mod_tables(c_all, w_ada, b_ada):
    depth, d, n = w_ada.shape
    tn = _pick(n, (512, 256, 128))
    return pl.pallas_call(
        _mod_body,
        out_shape=jax.ShapeDtypeStruct((depth, MOD_ROWS, n), F32),
        grid=(depth, n // tn),
        in_specs=[
            pl.BlockSpec((MOD_ROWS, d), lambda l, j: (0, 0)),
            pl.BlockSpec((None, d, tn), lambda l, j: (l, 0, j)),
            pl.BlockSpec((None, 1, tn), lambda l, j: (l, 0, j)),
        ],
        out_specs=pl.BlockSpec((None, MOD_ROWS, tn), lambda l, j: (l, 0, j)),
        compiler_params=_params("arbitrary", "arbitrary"),
        name="mod_tables",
    )(c_all, w_ada, b_ada.reshape(depth, 1, n))


def _chunk_batch(chunk_idx, n_prompt_chunks):
    return jnp.maximum(chunk_idx - (n_prompt_chunks - 1), 0)


def _top2(logits, n_experts):
    lane = lax.broadcasted_iota(jnp.int32, logits.shape, 1)
    lane_f = lane.astype(F32)
    neg = jnp.float32(-jnp.inf)
    lg = jnp.where(lane < n_experts, logits, neg)
    m1 = jnp.max(lg, axis=-1, keepdims=True)
    i1 = jnp.min(jnp.where(lg == m1, lane_f, float(ROUTE_LANES)), axis=-1, keepdims=True)
    lg2 = jnp.where(lane_f == i1, neg, lg)
    m2 = jnp.max(lg2, axis=-1, keepdims=True)
    i2 = jnp.min(jnp.where(lg2 == m2, lane_f, float(ROUTE_LANES)), axis=-1, keepdims=True)
    ex = jnp.exp(m2 - m1)
    den = 1.0 + ex
    weights = jnp.where(lane == 0, 1.0 / den, jnp.where(lane == 1, ex / den, 0.0))
    experts = jnp.where(lane == 0, i1, jnp.where(lane == 1, i2, 0.0)).astype(jnp.int32)
    return weights, experts


def _addnorm_body(*refs, tm, n_prompt_chunks, n_prompt_tiles, split_x, has_add, has_mod, n_experts):
    refs = list(refs)
    x_ref = refs.pop(0)
    xs_ref = refs.pop(0) if split_x else None
    if has_add:
        y_ref = refs.pop(0)
        gate_ref = refs.pop(0)
    g_ref = refs.pop(0)
    if has_mod:
        shift_ref = refs.pop(0)
        scale_ref = refs.pop(0)
    if n_experts:
        wr_ref = refs.pop(0)
    if has_add:
        xo_ref = refs.pop(0)
    if n_experts:
        re_ref = refs.pop()
        rw_ref = refs.pop()
    h_refs = refs
    i = pl.program_id(0)
    for c in range(tm // CHUNK):
        rows = pl.ds(c * CHUNK, CHUNK)
        b = _chunk_batch(i * (tm // CHUNK) + c, n_prompt_chunks)
        xv = x_ref[rows, :]
        if split_x:
            xv = jnp.where(i < n_prompt_tiles, xv, xs_ref[rows, :])
        if has_add:
            xv = xv + gate_ref[pl.ds(b, 1), :] * y_ref[rows, :]
            xo_ref[rows, :] = xv
        ms = jnp.mean(xv * xv, axis=-1, keepdims=True)
        hv = xv * lax.rsqrt(ms + RMS_EPS) * g_ref[...]
        if has_mod:
            hv = hv * (1.0 + scale_ref[pl.ds(b, 1), :]) + shift_ref[pl.ds(b, 1), :]
        for h_ref in h_refs:
            h_ref[rows, :] = hv.astype(h_ref.dtype)
        if n_experts:
            logits = jnp.dot(hv, wr_ref[...], preferred_element_type=F32,
                             precision=lax.Precision.HIGHEST)
            rw_ref[rows, :], re_ref[rows, :] = _top2(logits, n_experts)


def _addnorm(x, g, n_prompt_chunks, mod, *, y=None, gate=None, shift=None, scale=None,
             h_dtypes=(BF16,), w_router=None):
    split_x = isinstance(x, tuple)
    n_prompt = n_prompt_chunks * CHUNK
    t = sum(p.shape[0] for p in x) if split_x else x.shape[0]
    d = g.shape[0]
    tm = _pick(n_prompt, (256, 128, 64))
    assert t % tm == 0
    npt = n_prompt // tm
    has_add = y is not None
    has_mod = shift is not None
    n_experts = 0 if w_router is None else w_router.shape[1]
    row = pl.BlockSpec((tm, d), lambda i: (i, 0))

    def tab(where):
        layer, col = where
        return pl.BlockSpec((None, MOD_ROWS, d), lambda i: (layer, 0, col))

    if split_x:
        args = list(x)
        specs = [pl.BlockSpec((tm, d), lambda i: (jnp.minimum(i, npt - 1), 0)),
                 pl.BlockSpec((tm, d), lambda i: (jnp.maximum(i - npt, 0), 0))]
    else:
        args, specs = [x], [row]
    if has_add:
        args += [y, mod]
        specs += [row, tab(gate)]
    args.append(g.reshape(1, d))
    specs.append(pl.BlockSpec((1, d), lambda i: (0, 0)))
    if has_mod:
        args += [mod, mod]
        specs += [tab(shift), tab(scale)]
    if n_experts:
        args.append(jnp.pad(w_router, ((0, 0), (0, ROUTE_LANES - n_experts))))
        specs.append(pl.BlockSpec((d, ROUTE_LANES), lambda i: (0, 0)))
    out_shape, out_specs = [], []
    if has_add:
        out_shape.append(jax.ShapeDtypeStruct((t, d), F32))
        out_specs.append(row)
    for dt in h_dtypes:
        out_shape.append(jax.ShapeDtypeStruct((t, d), dt))
        out_specs.append(row)
    if n_experts:
        for dt in (F32, jnp.int32):
            out_shape.append(jax.ShapeDtypeStruct((t, ROUTE_LANES), dt))
            out_specs.append(pl.BlockSpec((tm, ROUTE_LANES), lambda i: (i, 0)))
    body = functools.partial(_addnorm_body, tm=tm, n_prompt_chunks=n_prompt_chunks, n_prompt_tiles=npt,
                             split_x=split_x, has_add=has_add, has_mod=has_mod, n_experts=n_experts)
    return pl.pallas_call(
        body, out_shape=out_shape, grid=(t // tm,), in_specs=specs, out_specs=out_specs,
        compiler_params=_params("arbitrary"), name="addnorm",
    )(*args)


def _mm_body(a_ref, b_ref, *refs, has_bias):
    o_ref = refs[-1]
    acc = jnp.dot(a_ref[...], b_ref[...], preferred_element_type=F32)
    if has_bias:
        acc = acc + refs[0][...]
    o_ref[...] = acc


def _matmul(a, b, bias=None):
    t, k = a.shape
    n = b.shape[1]
    tm = _pick(t, (768, 512, 256, 128, 64))
    tn = _pick(n, (1024, 512, 256, 128))
    args = [a, b]
    specs = [pl.BlockSpec((tm, k), lambda i, j: (i, 0)), pl.BlockSpec((k, tn), lambda i, j: (0, j))]
    if bias is not None:
        args.append(bias.reshape(1, n))
        specs.append(pl.BlockSpec((1, tn), lambda i, j: (0, j)))
    return pl.pallas_call(
        functools.partial(_mm_body, has_bias=bias is not None),
        out_shape=jax.ShapeDtypeStruct((t, n), F32),
        grid=(t // tm, n // tn), in_specs=specs,
        out_specs=pl.BlockSpec((tm, tn), lambda i, j: (i, j)),
        compiler_params=_params("arbitrary", "arbitrary"), name="matmul",
    )(*args)


def _glu_body(a_ref, ba_ref, bg_ref, bias_a_ref, bias_g_ref, o_ref):
    a = a_ref[...]
    za = jnp.dot(a, ba_ref[...], preferred_element_type=F32) + bias_a_ref[...]
    zg = jnp.dot(a, bg_ref[...], preferred_element_type=F32) + bias_g_ref[...]
    o_ref[...] = za * jax.nn.sigmoid(zg)


def _matmul_glu(a, b, bias):
    t, k = a.shape
    n = b.shape[1] // 2
    tm = _pick(t, (768, 512, 256, 128, 64))
    tn = _pick(n, (512, 256, 128))
    nb = n // tn
    bias = bias.reshape(1, 2 * n)
    return pl.pallas_call(
        _glu_body,
        out_shape=jax.ShapeDtypeStruct((t, n), F32),
        grid=(t // tm, nb),
        in_specs=[
            pl.BlockSpec((tm, k), lambda i, j: (i, 0)),
            pl.BlockSpec((k, tn), lambda i, j: (0, j)),
            pl.BlockSpec((k, tn), lambda i, j: (0, j + nb)),
            pl.BlockSpec((1, tn), lambda i, j: (0, j)),
            pl.BlockSpec((1, tn), lambda i, j: (0, j + nb)),
        ],
        out_specs=pl.BlockSpec((tm, tn), lambda i, j: (i, j)),
        compiler_params=_params("arbitrary", "arbitrary"), name="matmul_glu",
    )(a, b, b, bias, bias)


def _seq0_compute(pos_base, ph, uh, p_ref, bg_ref, cg_ref, v_ref, wgrp_ref, scale_ref, wsc_ref,
                  o_ref, utail_ref, pext, uext, *, tm, d_pool):
    n_groups = len(POOL_WINDOWS)
    gw = d_pool // n_groups
    pext[pl.ds(0, POOL_HIST_PAD), :] = ph
    pext[pl.ds(POOL_HIST_PAD, tm), :] = p_ref[...]
    u = cg_ref[...] * v_ref[...]
    uext[pl.ds(0, SCONV_HIST_PAD), :] = uh
    uext[pl.ds(SCONV_HIST_PAD, tm), :] = u
    utail_ref[...] = u[tm - SCONV_HIST_PAD:, :]

    pos = pos_base + lax.broadcasted_iota(jnp.int32, (tm, 1), 0)
    for gi, w in enumerate(POOL_WINDOWS):
        cols = pl.ds(gi * gw, gw)
        cur = pext[pl.ds(POOL_HIST_PAD, tm), cols]
        s = cur
        for k in range(1, w):
            s = s + pext[pl.ds(POOL_HIST_PAD - k, tm), cols]
        cnt = jnp.minimum(w, pos + 1).astype(F32)
        dlt = (s / cnt - cur).astype(BF16)
        ya = jnp.dot(dlt, wgrp_ref[gi], preferred_element_type=F32) * scale_ref[:, cols]
        o_ref[:, cols] = ya.astype(o_ref.dtype)

    conv = (uext[pl.ds(SCONV_HIST_PAD - 2, tm), :] * wsc_ref[pl.ds(0, 1), :]
            + uext[pl.ds(SCONV_HIST_PAD - 1, tm), :] * wsc_ref[pl.ds(1, 1), :]
            + uext[pl.ds(SCONV_HIST_PAD, tm), :] * wsc_ref[pl.ds(2, 1), :])
    o_ref[:, pl.ds(d_pool, d_pool)] = (bg_ref[...] * conv).astype(o_ref.dtype)


def _seq0_prompt_body(p_ref, bg_ref, cg_ref, v_ref, ph_ref, cgh_ref, vh_ref, wgrp_ref, scale_ref,
                      wsc_ref, o_ref, utail_ref, pext, uext, *, tm, d_pool, n_tiles):
    i = pl.program_id(0)

    @pl.when(i < n_tiles)
    def _():
        live = (i > 0).astype(F32)
        ph = ph_ref[...] * live
        uh = cgh_ref[...] * vh_ref[...] * live
        _seq0_compute(i * tm, ph, uh, p_ref, bg_ref, cg_ref, v_ref, wgrp_ref, scale_ref, wsc_ref,
                      o_ref, utail_ref, pext, uext, tm=tm, d_pool=d_pool)

    @pl.when(i >= n_tiles)
    def _():
        o_ref[...] = jnp.zeros_like(o_ref)


def _seq0_sample_body(p_ref, bg_ref, cg_ref, v_ref, ph_ref, uh_ref, wgrp_ref, scale_ref, wsc_ref,
                      prev_ref, o_ref, utail_ref, pext, uext, *, tm, d_pool):
    del prev_ref
    _seq0_compute(PAST_LEN, ph_ref[...], uh_ref[...], p_ref, bg_ref, cg_ref, v_ref, wgrp_ref,
                  scale_ref, wsc_ref, o_ref, utail_ref, pext, uext, tm=tm, d_pool=d_pool)


def _seq0(z, n_prompt, state_pool, state_sconv, w_grp, pool_scale, w_sconv):
    t = z.shape[0]
    n_batch, _, d_pool = state_pool.shape
    tm = _pick(n_prompt, (256, 128, 64))
    assert t % tm == 0
    n_tiles = n_prompt // tm
    scratch = lambda rows: [pltpu.VMEM((POOL_HIST_PAD + rows, d_pool), F32),
                            pltpu.VMEM((SCONV_HIST_PAD + rows, d_pool), F32)]
    consts = [w_grp, pool_scale.reshape(1, d_pool), w_sconv]
    const_specs = [pl.BlockSpec(w_grp.shape, lambda i: (0, 0, 0)),
                   pl.BlockSpec((1, d_pool), lambda i: (0, 0)),
                   pl.BlockSpec(w_sconv.shape, lambda i: (0, 0))]
    col = lambda c: pl.BlockSpec((tm, d_pool), lambda i: (i, c))
    hist = lambda rows, c: pl.BlockSpec(
        (rows, d_pool), lambda i: (jnp.maximum(i * (tm // rows) - 1, 0), c))
    y_p, utail_p = pl.pallas_call(
        functools.partial(_seq0_prompt_body, tm=tm, d_pool=d_pool, n_tiles=n_tiles),
        out_shape=[jax.ShapeDtypeStruct((t, 2 * d_pool), BF16),
                   jax.ShapeDtypeStruct((n_tiles * SCONV_HIST_PAD, d_pool), F32)],
        grid=(t // tm,),
        in_specs=[col(0), col(1), col(2), col(3), hist(POOL_HIST_PAD, 0),
                  hist(SCONV_HIST_PAD, 2), hist(SCONV_HIST_PAD, 3)] + const_specs,
        out_specs=[pl.BlockSpec((tm, 2 * d_pool), lambda i: (i, 0)),
                   pl.BlockSpec((SCONV_HIST_PAD, d_pool), lambda i: (jnp.minimum(i, n_tiles - 1), 0))],
        scratch_shapes=scratch(tm),
        compiler_params=_params("arbitrary"), name="seq0_prompt",
    )(z, z, z, z, z, z, z, *consts)

    c0 = n_prompt // CHUNK
    ph = jnp.pad(state_pool, ((0, 0), (POOL_HIST_PAD - state_pool.shape[1], 0), (0, 0)))
    uh = jnp.pad(state_sconv, ((0, 0), (SCONV_HIST_PAD - state_sconv.shape[1], 0), (0, 0)))
    scol = lambda c: pl.BlockSpec((CHUNK, d_pool), lambda i: (c0 + i, c))
    y, utail_s = pl.pallas_call(
        functools.partial(_seq0_sample_body, tm=CHUNK, d_pool=d_pool),
        out_shape=[jax.ShapeDtypeStruct((t, 2 * d_pool), BF16),
                   jax.ShapeDtypeStruct((n_batch * SCONV_HIST_PAD, d_pool), F32)],
        grid=(n_batch,),
        in_specs=[scol(0), scol(1), scol(2), scol(3),
                  pl.BlockSpec((None, POOL_HIST_PAD, d_pool), lambda i: (i, 0, 0)),
                  pl.BlockSpec((None, SCONV_HIST_PAD, d_pool), lambda i: (i, 0, 0))] + const_specs
        + [pl.BlockSpec(memory_space=pl.ANY)],
        out_specs=[pl.BlockSpec((CHUNK, 2 * d_pool), lambda i: (c0 + i, 0)),
                   pl.BlockSpec((SCONV_HIST_PAD, d_pool), lambda i: (i, 0))],
        scratch_shapes=scratch(CHUNK),
        input_output_aliases={9: 0},
        compiler_params=_params("arbitrary"), name="seq0_sample",
    )(z, z, z, z, ph, uh, *consts, y_p)
    return y, utail_p, utail_s


def _seq1_compute(uh, u_ref, wdw_ref, bdw_ref, lng_ref, lnb_ref, o_ref, uext, vbuf, *, tm, width):
    d = u_ref.shape[1]
    uext[pl.ds(0, DWCONV_HIST_PAD), :] = uh
    uext[pl.ds(DWCONV_HIST_PAD, tm), :] = u_ref[...]
    base = DWCONV_HIST_PAD - (width - 1)
    assert base >= 0 and width - 1 + base <= DWCONV_HIST_PAD
    cw = LANES
    rb = _pick(tm, (128, 64))

    def col_block(cb, carry):
        cols = pl.ds(pl.multiple_of(cb * cw, cw), cw)
        for r in range(tm // rb):
            acc = jnp.broadcast_to(bdw_ref[:, cols], (rb, cw))
            for res in range(SUBLANES):
                taps = [k for k in range(width) if (base + k) % SUBLANES == res]
                n = rb if res == 0 else rb + SUBLANES
                part = None
                for k in taps:
                    term = uext[pl.ds(r * rb + base + k - res, n), cols] * wdw_ref[pl.ds(k, 1), cols]
                    part = term if part is None else part + term
                if part is not None:
                    acc = acc + part[res:res + rb]
            vbuf[pl.ds(r * rb, rb), cols] = acc
        return carry

    lax.fori_loop(0, d // cw, col_block, 0)
    rb = CHUNK
    for r in range(tm // rb):
        rows = pl.ds(r * rb, rb)
        v = vbuf[rows, :]
        mu = jnp.mean(v, axis=-1, keepdims=True)
        vc = v - mu
        var = jnp.mean(vc * vc, axis=-1, keepdims=True)
        y = vc * lax.rsqrt(var + LN_EPS) * lng_ref[...] + lnb_ref[...]
        o_ref[rows, :] = (y * jax.nn.sigmoid(y)).astype(o_ref.dtype)


def _seq1_prompt_body(u_ref, uh_ref, wdw_ref, bdw_ref, lng_ref, lnb_ref, o_ref, uext, vbuf,
                      *, tm, width, n_tiles):
    i = pl.program_id(0)

    @pl.when(i < n_tiles)
    def _():
        live = (i > 0).astype(F32)
        _seq1_compute(uh_ref[...] * live, u_ref, wdw_ref, bdw_ref, lng_ref, lnb_ref, o_ref, uext, vbuf,
                      tm=tm, width=width)

    @pl.when(i >= n_tiles)
    def _():
        o_ref[...] = jnp.zeros_like(o_ref)


def _seq1_sample_body(u_ref, uh_ref, wdw_ref, bdw_ref, lng_ref, lnb_ref, prev_ref, o_ref, uext, vbuf,
                      *, tm, width):
    del prev_ref
    _seq1_compute(uh_ref[...], u_ref, wdw_ref, bdw_ref, lng_ref, lnb_ref, o_ref, uext, vbuf,
                  tm=tm, width=width)


def _seq1(u, n_prompt, state_dwconv, w_dw, b_dw, ln_g, ln_b):
    t, d = u.shape
    n_batch = state_dwconv.shape[0]
    width = w_dw.shape[0]
    tm = _pick(n_prompt, (256, 128, 64))
    assert t % tm == 0
    consts = [w_dw, b_dw.reshape(1, d), ln_g.reshape(1, d), ln_b.reshape(1, d)]
    const_specs = [pl.BlockSpec(w_dw.shape, lambda i: (0, 0))] + [
        pl.BlockSpec((1, d), lambda i: (0, 0)) for _ in range(3)]
    scratch = lambda rows: [pltpu.VMEM((DWCONV_HIST_PAD + rows, d), F32), pltpu.VMEM((rows, d), F32)]
    v_p = pl.pallas_call(
        functools.partial(_seq1_prompt_body, tm=tm, width=width, n_tiles=n_prompt // tm),
        out_shape=jax.ShapeDtypeStruct((t, d), BF16),
        grid=(t // tm,),
        in_specs=[pl.BlockSpec((tm, d), lambda i: (i, 0)),
                  pl.BlockSpec((DWCONV_HIST_PAD, d),
                               lambda i: (jnp.maximum(i * (tm // DWCONV_HIST_PAD) - 1, 0), 0))]
        + const_specs,
        out_specs=pl.BlockSpec((tm, d), lambda i: (i, 0)),
        scratch_shapes=scratch(tm),
        compiler_params=_params("arbitrary"), name="seq1_prompt",
    )(u, u, *consts)
    c0 = n_prompt // CHUNK
    uh = jnp.pad(state_dwconv, ((0, 0), (DWCONV_HIST_PAD - state_dwconv.shape[1], 0), (0, 0)))
    return pl.pallas_call(
        functools.partial(_seq1_sample_body, tm=CHUNK, width=width),
        out_shape=jax.ShapeDtypeStruct((t, d), BF16),
        grid=(n_batch,),
        in_specs=[pl.BlockSpec((CHUNK, d), lambda i: (c0 + i, 0)),
                  pl.BlockSpec((None, DWCONV_HIST_PAD, d), lambda i: (i, 0, 0))] + const_specs
        + [pl.BlockSpec(memory_space=pl.ANY)],
        out_specs=pl.BlockSpec((CHUNK, d), lambda i: (c0 + i, 0)),
        scratch_shapes=scratch(CHUNK),
        input_output_aliases={6: 0},
        compiler_params=_params("arbitrary"), name="seq1_sample",
    )(u, uh, *consts, v_p)


def _ffn_body(te_ref, nv_ref, h_ref, wg_ref, wu_ref, wd_ref, *refs, n_chunk, n_cast):
    del te_ref
    cast_in, o_ref, cast_out = refs[:n_cast], refs[n_cast], refs[n_cast + 1:]
    i = pl.program_id(0)
    j = pl.program_id(1)

    @pl.when(j == 0)
    def _():
        o_ref[...] = jnp.zeros_like(o_ref)

    @pl.when(i < nv_ref[0])
    def _():
        for src, dst in zip(cast_in, cast_out):
            dst[...] = src[...].astype(dst.dtype)
        h = h_ref[...]
        g = jnp.dot(h, wg_ref[...], preferred_element_type=F32)
        u = jnp.dot(h, wu_ref[...], preferred_element_type=F32)
        a = (g * jax.nn.sigmoid(g) * u).astype(BF16)
        d = o_ref.shape[1]
        for c in range(d // n_chunk):
            cols = pl.ds(c * n_chunk, n_chunk)
            o_ref[:, cols] += jnp.dot(a, wd_ref[:, cols], preferred_element_type=F32)


def _ffn(h, w_gate, w_up, w_down, tile_set, n_valid, *, tm, tf, cast=()):
    p, d = h.shape
    f = w_gate.shape[2]
    n_tiles, n_f = p // tm, f // tf
    n_steps = n_tiles * n_f
    cast_specs, cast_shapes = [], []
    for w in cast:
        rows = w.shape[0]
        rb = next(r for r in (16, 32, 64, 128, 256, 512, 1024, 2048, 4096)
                  if rows % r == 0 and rows // r <= n_steps)
        nblk = rows // rb
        spec = pl.BlockSpec((rb, w.shape[1]),
                            lambda i, j, te, nv, nblk=nblk: (jnp.minimum(i * n_f + j, nblk - 1), 0))
        cast_specs.append(spec)
        cast_shapes.append(jax.ShapeDtypeStruct(w.shape, BF16))

    def row_map(i, j, te, nv):
        return (jnp.minimum(i, nv[0] - 1), 0)

    def _ij(i, j, te, nv):
        live = i < nv[0]
        return te[jnp.minimum(i, nv[0] - 1)], jnp.where(live, j, n_f - 1)

    def up_map(i, j, te, nv):
        e, jj = _ij(i, j, te, nv)
        return (e, 0, jj)

    def down_map(i, j, te, nv):
        e, jj = _ij(i, j, te, nv)
        return (e, jj, 0)

    grid_spec = pltpu.PrefetchScalarGridSpec(
        num_scalar_prefetch=2, grid=(n_tiles, n_f),
        in_specs=[pl.BlockSpec((tm, d), row_map),
                  pl.BlockSpec((None, d, tf), up_map),
                  pl.BlockSpec((None, d, tf), up_map),
                  pl.BlockSpec((None, tf, d), down_map)] + cast_specs,
        out_specs=[pl.BlockSpec((tm, d), lambda i, j, te, nv: (i, 0))] + cast_specs)
    return pl.pallas_call(
        functools.partial(_ffn_body, n_chunk=_pick(d, (1024, 512, 256, 128)), n_cast=len(cast)),
        out_shape=[jax.ShapeDtypeStruct((p, d), F32)] + cast_shapes, grid_spec=grid_spec,
        compiler_params=_params("arbitrary", "arbitrary"), name="ffn",
    )(tile_set, n_valid, h, w_gate, w_up, w_down, *cast)


def _row_copy(src_hbm, row, dst, r, sem):
    return pltpu.make_async_copy(src_hbm.at[pl.ds(row, 1), :], dst.at[pl.ds(r, 1), :], sem)


def _ring_step(n_steps, start_rows, wait_rows):
    i = pl.program_id(0)
    slot = lax.rem(i, RING_SLOTS)

    @pl.when(i == 0)
    def _():
        start_rows(False, 0)

    @pl.when(i + 1 < n_steps)
    def _():
        start_rows(True, 1 - slot)

    wait_rows(slot)
    return slot


def _gather_body(src_ref, src_next_ref, h_hbm, o_ref, buf, sem, *, rows, n_steps):
    def start_rows(is_next, slot):
        idx = src_next_ref if is_next else src_ref

        def start(r, c):
            _row_copy(h_hbm, idx[0, 0, r], buf.at[slot], r, sem.at[slot]).start()
            return c

        lax.fori_loop(0, rows, start, 0, unroll=DMA_UNROLL)

    def wait_rows(slot):
        def wait(r, c):
            _row_copy(h_hbm, 0, buf.at[slot], r, sem.at[slot]).wait()
            return c

        lax.fori_loop(0, rows, wait, 0, unroll=DMA_UNROLL)

    slot = _ring_step(n_steps, start_rows, wait_rows)
    o_ref[...] = buf[slot].astype(o_ref.dtype)


def _gather_rows(h, src, *, rows):
    d = h.shape[1]
    p = src.shape[0]
    n = p // rows
    src = src.reshape(n, 1, rows)
    idx_spec = lambda step: pl.BlockSpec((1, 1, rows), lambda i: (jnp.minimum(i + step, n - 1), 0, 0),
                                         memory_space=pltpu.SMEM)
    return pl.pallas_call(
        functools.partial(_gather_body, rows=rows, n_steps=n),
        out_shape=jax.ShapeDtypeStruct((p, d), BF16),
        grid=(n,),
        in_specs=[idx_spec(0), idx_spec(1), pl.BlockSpec(memory_space=pl.ANY)],
        out_specs=pl.BlockSpec((rows, d), lambda i: (i, 0)),
        scratch_shapes=[pltpu.VMEM((RING_SLOTS, rows, d), F32), pltpu.SemaphoreType.DMA((RING_SLOTS,))],
        compiler_params=_params("arbitrary"), name="gather_rows",
    )(src, src, h)


def _combine_body(pos_ref, pos_next_ref, cw_ref, x_ref, gate_ref, g_ref, y_hbm, op_ref, os_ref, buf, sem,
                  *, rows, n_steps, n_prompt_tiles, n_prompt_chunks):
    i = pl.program_id(0)

    def start_rows(is_next, slot):
        idx = pos_next_ref if is_next else pos_ref

        def start(r, c):
            for k in range(TOP_K):
                _row_copy(y_hbm, idx[0, 0, TOP_K * r + k], buf.at[slot, k], r, sem.at[slot]).start()
            return c

        lax.fori_loop(0, rows, start, 0, unroll=DMA_UNROLL)

    def wait_rows(slot):
        def wait(r, c):
            for k in range(TOP_K):
                _row_copy(y_hbm, 0, buf.at[slot, k], r, sem.at[slot]).wait()
            return c

        lax.fori_loop(0, rows, wait, 0, unroll=DMA_UNROLL)

    slot = _ring_step(n_steps, start_rows, wait_rows)

    def emit(o_ref):
        for c in range(rows // CHUNK):
            rs = pl.ds(c * CHUNK, CHUNK)
            b = _chunk_batch(i * (rows // CHUNK) + c, n_prompt_chunks)
            cw = cw_ref[rs, :]
            f = cw[:, 0:1] * buf[slot, 0, rs, :] + cw[:, 1:2] * buf[slot, 1, rs, :]
            xv = x_ref[rs, :] + gate_ref[pl.ds(b, 1), :] * f
            ms = jnp.mean(xv * xv, axis=-1, keepdims=True)
            o_ref[rs, :] = xv * lax.rsqrt(ms + RMS_EPS) * g_ref[...]

    @pl.when(i < n_prompt_tiles)
    def _():
        emit(op_ref)

    @pl.when(i >= n_prompt_tiles)
    def _():
        emit(os_ref)


def _combine(x, y_sorted, pos, cw, mod, gate, g_final, n_prompt):
    t, d = x.shape
    rows = _pick(n_prompt, (256, 128, 64))
    n = t // rows
    npt = n_prompt // rows
    layer, gate_col = gate
    pos = pos.reshape(n, 1, TOP_K * rows)
    idx_spec = lambda step: pl.BlockSpec((1, 1, TOP_K * rows),
                                         lambda i: (jnp.minimum(i + step, n - 1), 0, 0),
                                         memory_space=pltpu.SMEM)
    return pl.pallas_call(
        functools.partial(_combine_body, rows=rows, n_steps=n, n_prompt_tiles=npt,
                          n_prompt_chunks=n_prompt // CHUNK),
        out_shape=[jax.ShapeDtypeStruct((n_prompt, d), F32),
                   jax.ShapeDtypeStruct((t - n_prompt, d), F32)],
        grid=(n,),
        in_specs=[idx_spec(0), idx_spec(1),
                  pl.BlockSpec((rows, ROUTE_LANES), lambda i: (i, 0)),
                  pl.BlockSpec((rows, d), lambda i: (i, 0)),
                  pl.BlockSpec((None, MOD_ROWS, d), lambda i: (layer, 0, gate_col)),
                  pl.BlockSpec((1, d), lambda i: (0, 0)),
                  pl.BlockSpec(memory_space=pl.ANY)],
        out_specs=[pl.BlockSpec((rows, d), lambda i: (jnp.minimum(i, npt - 1), 0)),
                   pl.BlockSpec((rows, d), lambda i: (jnp.maximum(i - npt, 0), 0))],
        scratch_shapes=[pltpu.VMEM((RING_SLOTS, TOP_K, rows, d), F32),
                        pltpu.SemaphoreType.DMA((RING_SLOTS,))],
        compiler_params=_params("arbitrary"), name="combine",
    )(pos, pos, cw, x, mod, g_final.reshape(1, d), y_sorted)


def _route_plan(e_idx, n_experts, tm, n_tiles):
    t = e_idx.shape[0]
    flat_e = e_idx.reshape(-1)
    onehot = (flat_e[:, None] == jnp.arange(n_experts)[None, :]).astype(jnp.int32)
    csum = jnp.cumsum(onehot, axis=0)
    rank = jnp.take_along_axis(csum, flat_e[:, None], axis=1)[:, 0] - 1
    counts = csum[-1]
    tiles_per = (counts + tm - 1) // tm
    tile_end = jnp.cumsum(tiles_per)
    row_start = (tile_end - tiles_per) * tm
    pos = row_start[flat_e] + rank
    src = jnp.zeros((n_tiles * tm,), jnp.int32).at[pos].set(jnp.arange(TOP_K * t, dtype=jnp.int32) // TOP_K)
    before = (tile_end[None, :] <= jnp.arange(n_tiles)[:, None]).astype(jnp.int32)
    tile_set = jnp.minimum(jnp.sum(before, axis=1), n_experts - 1).astype(jnp.int32)
    n_valid = tile_end[-1:].astype(jnp.int32)
    return pos.astype(jnp.int32).reshape(t, TOP_K), src, tile_set, n_valid


def kernel(x_prompt, x_sample, state_pool, state_shortconv, state_dwconv, c_prompt, c_sample, w_ada, b_ada, g_mix, g_ffn, g_final, w_in_ab, w_pool_grp, pool_scale, w_sconv, w_out_ab, w_ffn_gate, w_ffn_up, w_ffn_down, w_pw1, b_pw1, w_dw, b_dw, ln_g, ln_b, w_pw2, b_pw2, w_router, w_exp_gate, w_exp_up, w_exp_down):
    n_pb, seq, d = x_prompt.shape
    n_sb, dec_seq, _ = x_sample.shape
    assert n_pb == 1 and dec_seq == CHUNK and seq % CHUNK == 0
    assert 1 + n_sb <= MOD_ROWS and w_ada.shape[0] == 2
    n_prompt = seq
    t = n_prompt + n_sb * dec_seq
    npc = n_prompt // CHUNK
    d_pool = state_pool.shape[-1]
    n_experts = w_router.shape[-1]

    x = (x_prompt.reshape(n_prompt, d), x_sample.reshape(n_sb * dec_seq, d))
    c_all = jnp.concatenate([c_prompt, c_sample, jnp.zeros((MOD_ROWS - 1 - n_sb, d), F32)], axis=0)
    mod = _mod_tables(c_all, w_ada, b_ada)
    bf = lambda w: w.astype(BF16)

    (h,) = _addnorm(x, g_mix[0], npc, mod, shift=(0, 0), scale=(0, 1))
    z = _matmul(h, bf(w_in_ab[0]))
    y_cat, utail_p, utail_s = _seq0(z, n_prompt, state_pool[0], state_shortconv[0],
                                    bf(w_pool_grp[0]), pool_scale[0], w_sconv[0])
    y = _matmul(y_cat, bf(w_out_ab[0]))
    x, h = _addnorm(x, g_ffn[0], npc, mod, y=y, gate=(0, 2), shift=(0, 3), scale=(0, 4))
    tm_d = _pick(t, (512, 256, 128, 64))
    n_td = t // tm_d
    d_ffe = w_exp_gate.shape[3]
    f, wg_e, wu_e = _ffn(h, bf(w_ffn_gate), bf(w_ffn_up), bf(w_ffn_down),
                         jnp.zeros((n_td,), jnp.int32), jnp.full((1,), n_td, jnp.int32),
                         tm=tm_d, tf=_pick(w_ffn_gate.shape[2], (256, 128)),
                         cast=(w_exp_gate[0].reshape(n_experts * d, d_ffe),
                               w_exp_up[0].reshape(n_experts * d, d_ffe)))
    wg_e = wg_e.reshape(n_experts, d, d_ffe)
    wu_e = wu_e.reshape(n_experts, d, d_ffe)

    x, h = _addnorm(x, g_mix[1], npc, mod, y=f, gate=(0, 5), shift=(1, 0), scale=(1, 1))
    u = _matmul_glu(h, bf(w_pw1[0]), b_pw1[0])
    v = _seq1(u, n_prompt, state_dwconv[0], w_dw[0], b_dw[0], ln_g[0], ln_b[0])
    y = _matmul(v, bf(w_pw2[0]), b_pw2[0])
    x, h32, cw, e_idx = _addnorm(x, g_ffn[1], npc, mod, y=y, gate=(1, 2), shift=(1, 3), scale=(1, 4),
                                 h_dtypes=(F32,), w_router=w_router[0])
    tm_e = _pick(TOP_K * t, (512, 256, 128, 64))
    n_te = TOP_K * t // tm_e + n_experts
    pos, src, tile_set, n_valid = _route_plan(e_idx[:, :TOP_K], n_experts, tm_e, n_te)
    h_sorted = _gather_rows(h32, src, rows=tm_e)
    (y_sorted,) = _ffn(h_sorted, wg_e, wu_e, bf(w_exp_down[0]), tile_set, n_valid,
                       tm=tm_e, tf=_pick(d_ffe, (512, 256, 128)))
    y_p, y_s = _combine(x, y_sorted, pos, cw, mod, (1, 5), g_final, n_prompt)

    n_tiles0 = utail_p.shape[0] // SCONV_HIST_PAD
    sconv_keep = state_shortconv.shape[2]
    dw_keep = state_dwconv.shape[2]
    zs = z[n_prompt:].reshape(n_sb, dec_seq, -1)
    us = u[n_prompt:].reshape(n_sb, dec_seq, -1)
    new_pool_p = z[n_prompt - POOL_HIST:n_prompt, :d_pool][None, None]
    new_pool_s = zs[:, dec_seq - POOL_HIST:, :d_pool][None]
    new_sconv_p = utail_p[(n_tiles0 - 1) * SCONV_HIST_PAD:][SCONV_HIST_PAD - sconv_keep:][None, None]
    new_sconv_s = utail_s.reshape(n_sb, SCONV_HIST_PAD, -1)[:, SCONV_HIST_PAD - sconv_keep:][None]
    new_dw_p = u[n_prompt - dw_keep:n_prompt][None, None]
    new_dw_s = us[:, dec_seq - dw_keep:][None]
    return (y_p.reshape(1, n_prompt, d), y_s.reshape(n_sb, dec_seq, d),
            new_pool_p, new_sconv_p, new_dw_p, new_pool_s, new_sconv_s, new_dw_s)
```

```python
import functools

import jax
import jax.numpy as jnp
from jax import lax
from jax.experimental import pallas as pl
from jax.experimental.pallas import tpu as pltpu

F32 = jnp.float32
BF16 = jnp.bfloat16

CHUNK = 64
POOL_WINDOWS = (2, 4, 8, 16)
POOL_HIST = max(POOL_WINDOWS) - 1
POOL_HIST_PAD = 16
SCONV_HIST_PAD = 8
DWCONV_HIST_PAD = 32
PAST_LEN = 4096
TOP_K = 2
RMS_EPS = 1e-6
LN_EPS = 1e-5
MOD_ROWS = 16
RING_SLOTS = 2
DMA_UNROLL = 8
DMA_PRIORITIES = 2
SUBLANES = 8
LANES = 128
ROUTE_LANES = LANES
V7X_VMEM_BYTES = 64 * 1024 * 1024
VMEM_LIMIT = V7X_VMEM_BYTES - 6 * 1024 * 1024


def _params(*sem):
    return pltpu.CompilerParams(dimension_semantics=sem, vmem_limit_bytes=VMEM_LIMIT)


def _pick(n, prefs):
    for p in prefs:
        if n % p == 0:
            return p
    return n


def _mod_body(c_ref, w_ref, b_ref, o_ref):
    c = c_ref[...]
    sc = (c * jax.nn.sigmoid(c)).astype(BF16)
    o_ref[...] = jnp.dot(sc, w_ref[...].astype(BF16), preferred_element_type=F32) + b_ref[...]


def _mod_tables(c_all, w_ada, b_ada):
    depth, d, n = w_ada.shape
    tn = _pick(n, (512, 256, 128))
    return pl.pallas_call(
        _mod_body,
        out_shape=jax.ShapeDtypeStruct((depth, MOD_ROWS, n), F32),
        grid=(depth, n // tn),
        in_specs=[
            pl.BlockSpec((MOD_ROWS, d), lambda l, j: (0, 0)),
            pl.BlockSpec((None, d, tn), lambda l, j: (l, 0, j)),
            pl.BlockSpec((None, 1, tn), lambda l, j: (l, 0, j)),
        ],
        out_specs=pl.BlockSpec((None, MOD_ROWS, tn), lambda l, j: (l, 0, j)),
        compiler_params=_params("arbitrary", "arbitrary"),
        name="mod_tables",
    )(c_all, w_ada, b_ada.reshape(depth, 1, n))


def _chunk_batch(chunk_idx, n_prompt_chunks):
    return jnp.maximum(chunk_idx - (n_prompt_chunks - 1), 0)


def _top2(logits, n_experts):
    lane = lax.broadcasted_iota(jnp.int32, logits.shape, 1)
    lane_f = lane.astype(F32)
    neg = jnp.float32(-jnp.inf)
    lg = jnp.where(lane < n_experts, logits, neg)
    m1 = jnp.max(lg, axis=-1, keepdims=True)
    i1 = jnp.min(jnp.where(lg == m1, lane_f, float(ROUTE_LANES)), axis=-1, keepdims=True)
    lg2 = jnp.where(lane_f == i1, neg, lg)
    m2 = jnp.max(lg2, axis=-1, keepdims=True)
    i2 = jnp.min(jnp.where(lg2 == m2, lane_f, float(ROUTE_LANES)), axis=-1, keepdims=True)
    ex = jnp.exp(m2 - m1)
    den = 1.0 + ex
    weights = jnp.where(lane == 0, 1.0 / den, jnp.where(lane == 1, ex / den, 0.0))
    experts = jnp.where(lane == 0, i1, jnp.where(lane == 1, i2, 0.0)).astype(jnp.int32)
    return weights, experts


def _addnorm_body(*refs, tm, n_prompt_chunks, n_prompt_tiles, split_x, has_add, has_mod, n_experts):
    refs = list(refs)
    x_ref = refs.pop(0)
    xs_ref = refs.pop(0) if split_x else None
    if has_add:
        y_ref = refs.pop(0)
        gate_ref = refs.pop(0)
    g_ref = refs.pop(0)
    if has_mod:
        shift_ref = refs.pop(0)
        scale_ref = refs.pop(0)
    if n_experts:
        wr_ref = refs.pop(0)
    if has_add:
        xo_ref = refs.pop(0)
    if n_experts:
        re_ref = refs.pop()
        rw_ref = refs.pop()
    h_refs = refs
    i = pl.program_id(0)
    for c in range(tm // CHUNK):
        rows = pl.ds(c * CHUNK, CHUNK)
        b = _chunk_batch(i * (tm // CHUNK) + c, n_prompt_chunks)
        xv = x_ref[rows, :]
        if split_x:
            xv = jnp.where(i < n_prompt_tiles, xv, xs_ref[rows, :])
        if has_add:
            xv = xv + gate_ref[pl.ds(b, 1), :] * y_ref[rows, :]
            xo_ref[rows, :] = xv
        ms = jnp.mean(xv * xv, axis=-1, keepdims=True)
        hv = xv * lax.rsqrt(ms + RMS_EPS) * g_ref[...]
        if has_mod:
            hv = hv * (1.0 + scale_ref[pl.ds(b, 1), :]) + shift_ref[pl.ds(b, 1), :]
        for h_ref in h_refs:
            h_ref[rows, :] = hv.astype(h_ref.dtype)
        if n_experts:
            logits = jnp.dot(hv, wr_ref[...], preferred_element_type=F32,
                             precision=lax.Precision.HIGHEST)
            rw_ref[rows, :], re_ref[rows, :] = _top2(logits, n_experts)


def _addnorm(x, g, n_prompt_chunks, mod, *, y=None, gate=None, shift=None, scale=None,
             h_dtypes=(BF16,), w_router=None):
    split_x = isinstance(x, tuple)
    n_prompt = n_prompt_chunks * CHUNK
    t = sum(p.shape[0] for p in x) if split_x else x.shape[0]
    d = g.shape[0]
    tm = _pick(n_prompt, (256, 128, 64))
    assert t % tm == 0
    npt = n_prompt // tm
    has_add = y is not None
    has_mod = shift is not None
    n_experts = 0 if w_router is None else w_router.shape[1]
    row = pl.BlockSpec((tm, d), lambda i: (i, 0))

    def tab(where):
        layer, col = where
        return pl.BlockSpec((None, MOD_ROWS, d), lambda i: (layer, 0, col))

    if split_x:
        args = list(x)
        specs = [pl.BlockSpec((tm, d), lambda i: (jnp.minimum(i, npt - 1), 0)),
                 pl.BlockSpec((tm, d), lambda i: (jnp.maximum(i - npt, 0), 0))]
    else:
        args, specs = [x], [row]
    if has_add:
        args += [y, mod]
        specs += [row, tab(gate)]
    args.append(g.reshape(1, d))
    specs.append(pl.BlockSpec((1, d), lambda i: (0, 0)))
    if has_mod:
        args += [mod, mod]
        specs += [tab(shift), tab(scale)]
    if n_experts:
        args.append(jnp.pad(w_router, ((0, 0), (0, ROUTE_LANES - n_experts))))
        specs.append(pl.BlockSpec((d, ROUTE_LANES), lambda i: (0, 0)))
    out_shape, out_specs = [], []
    if has_add:
        out_shape.append(jax.ShapeDtypeStruct((t, d), F32))
        out_specs.append(row)
    for dt in h_dtypes:
        out_shape.append(jax.ShapeDtypeStruct((t, d), dt))
        out_specs.append(row)
    if n_experts:
        for dt in (F32, jnp.int32):
            out_shape.append(jax.ShapeDtypeStruct((t, ROUTE_LANES), dt))
            out_specs.append(pl.BlockSpec((tm, ROUTE_LANES), lambda i: (i, 0)))
    body = functools.partial(_addnorm_body, tm=tm, n_prompt_chunks=n_prompt_chunks, n_prompt_tiles=npt,
                             split_x=split_x, has_add=has_add, has_mod=has_mod, n_experts=n_experts)
    return pl.pallas_call(
        body, out_shape=out_shape, grid=(t // tm,), in_specs=specs, out_specs=out_specs,
        compiler_params=_params("arbitrary"), name="addnorm",
    )(*args)


def _cast_blocks(cast, n_steps, step_of):
    specs, shapes = [], []
    for w in cast:
        rows = w.shape[0]
        rb = next(r for r in (16, 32, 64, 128, 256, 512, 1024, 2048, 4096)
                  if rows % r == 0 and rows // r <= n_steps)
        nblk = rows // rb
        specs.append(pl.BlockSpec(
            (rb, w.shape[1]), lambda *ids, nblk=nblk: (jnp.minimum(step_of(*ids), nblk - 1), 0)))
        shapes.append(jax.ShapeDtypeStruct(w.shape, BF16))
    return specs, shapes


def _mm_body(a_ref, b_ref, *refs, has_bias, n_cast):
    refs = list(refs)
    bias_ref = refs.pop(0) if has_bias else None
    cast_in, o_ref, cast_out = refs[:n_cast], refs[n_cast], refs[n_cast + 1:]
    for src, dst in zip(cast_in, cast_out):
        dst[...] = src[...].astype(dst.dtype)
    acc = jnp.dot(a_ref[...], b_ref[...], preferred_element_type=F32)
    if has_bias:
        acc = acc + bias_ref[...]
    o_ref[...] = acc


def _matmul(a, b, bias=None, cast=()):
    t, k = a.shape
    n = b.shape[1]
    tm = _pick(t, (768, 512, 256, 128, 64))
    tn = _pick(n, (1024, 512, 256, 128))
    n_j = n // tn
    args = [a, b]
    specs = [pl.BlockSpec((tm, k), lambda i, j: (i, 0)), pl.BlockSpec((k, tn), lambda i, j: (0, j))]
    if bias is not None:
        args.append(bias.reshape(1, n))
        specs.append(pl.BlockSpec((1, tn), lambda i, j: (0, j)))
    cast_specs, cast_shapes = _cast_blocks(cast, (t // tm) * n_j, lambda i, j: i * n_j + j)
    out = pl.pallas_call(
        functools.partial(_mm_body, has_bias=bias is not None, n_cast=len(cast)),
        out_shape=[jax.ShapeDtypeStruct((t, n), F32)] + cast_shapes,
        grid=(t // tm, n_j), in_specs=specs + cast_specs,
        out_specs=[pl.BlockSpec((tm, tn), lambda i, j: (i, j))] + cast_specs,
        compiler_params=_params("arbitrary", "arbitrary"), name="matmul",
    )(*args, *cast)
    return out if cast else out[0]


def _glu_body(a_ref, ba_ref, bg_ref, bias_a_ref, bias_g_ref, o_ref):
    a = a_ref[...]
    za = jnp.dot(a, ba_ref[...], preferred_element_type=F32) + bias_a_ref[...]
    zg = jnp.dot(a, bg_ref[...], preferred_element_type=F32) + bias_g_ref[...]
    o_ref[...] = za * jax.nn.sigmoid(zg)


def _matmul_glu(a, b, bias):
    t, k = a.shape
    n = b.shape[1] // 2
    tm = _pick(t, (768, 512, 256, 128, 64))
    tn = _pick(n, (512, 256, 128))
    nb = n // tn
    bias = bias.reshape(1, 2 * n)
    return pl.pallas_call(
        _glu_body,
        out_shape=jax.ShapeDtypeStruct((t, n), F32),
        grid=(t // tm, nb),
        in_specs=[
            pl.BlockSpec((tm, k), lambda i, j: (i, 0)),
            pl.BlockSpec((k, tn), lambda i, j: (0, j)),
            pl.BlockSpec((k, tn), lambda i, j: (0, j + nb)),
            pl.BlockSpec((1, tn), lambda i, j: (0, j)),
            pl.BlockSpec((1, tn), lambda i, j: (0, j + nb)),
        ],
        out_specs=pl.BlockSpec((tm, tn), lambda i, j: (i, j)),
        compiler_params=_params("arbitrary", "arbitrary"), name="matmul_glu",
    )(a, b, b, bias, bias)


def _seq0_compute(pos_base, ph, uh, p_ref, bg_ref, cg_ref, v_ref, wgrp_ref, scale_ref, wsc_ref,
                  o_ref, utail_ref, pext, uext, *, tm, d_pool):
    n_groups = len(POOL_WINDOWS)
    gw = d_pool // n_groups
    pext[pl.ds(0, POOL_HIST_PAD), :] = ph
    pext[pl.ds(POOL_HIST_PAD, tm), :] = p_ref[...]
    u = cg_ref[...] * v_ref[...]
    uext[pl.ds(0, SCONV_HIST_PAD), :] = uh
    uext[pl.ds(SCONV_HIST_PAD, tm), :] = u
    utail_ref[...] = u[tm - SCONV_HIST_PAD:, :]

    pos = pos_base + lax.broadcasted_iota(jnp.int32, (tm, 1), 0)
    for gi, w in enumerate(POOL_WINDOWS):
        cols = pl.ds(gi * gw, gw)
        cur = pext[pl.ds(POOL_HIST_PAD, tm), cols]
        s = cur
        for k in range(1, w):
            s = s + pext[pl.ds(POOL_HIST_PAD - k, tm), cols]
        cnt = jnp.minimum(w, pos + 1).astype(F32)
        dlt = (s / cnt - cur).astype(BF16)
        ya = jnp.dot(dlt, wgrp_ref[gi], preferred_element_type=F32) * scale_ref[:, cols]
        o_ref[:, cols] = ya.astype(o_ref.dtype)

    conv = (uext[pl.ds(SCONV_HIST_PAD - 2, tm), :] * wsc_ref[pl.ds(0, 1), :]
            + uext[pl.ds(SCONV_HIST_PAD - 1, tm), :] * wsc_ref[pl.ds(1, 1), :]
            + uext[pl.ds(SCONV_HIST_PAD, tm), :] * wsc_ref[pl.ds(2, 1), :])
    o_ref[:, pl.ds(d_pool, d_pool)] = (bg_ref[...] * conv).astype(o_ref.dtype)


def _seq0_prompt_body(p_ref, bg_ref, cg_ref, v_ref, ph_ref, cgh_ref, vh_ref, wgrp_ref, scale_ref,
                      wsc_ref, o_ref, utail_ref, pext, uext, *, tm, d_pool, n_tiles):
    i = pl.program_id(0)

    @pl.when(i < n_tiles)
    def _():
        live = (i > 0).astype(F32)
        ph = ph_ref[...] * live
        uh = cgh_ref[...] * vh_ref[...] * live
        _seq0_compute(i * tm, ph, uh, p_ref, bg_ref, cg_ref, v_ref, wgrp_ref, scale_ref, wsc_ref,
                      o_ref, utail_ref, pext, uext, tm=tm, d_pool=d_pool)

    @pl.when(i >= n_tiles)
    def _():
        o_ref[...] = jnp.zeros_like(o_ref)


def _seq0_sample_body(p_ref, bg_ref, cg_ref, v_ref, ph_ref, uh_ref, wgrp_ref, scale_ref, wsc_ref,
                      prev_ref, o_ref, utail_ref, pext, uext, *, tm, d_pool):
    del prev_ref
    _seq0_compute(PAST_LEN, ph_ref[...], uh_ref[...], p_ref, bg_ref, cg_ref, v_ref, wgrp_ref,
                  scale_ref, wsc_ref, o_ref, utail_ref, pext, uext, tm=tm, d_pool=d_pool)


def _seq0(z, n_prompt, state_pool, state_sconv, w_grp, pool_scale, w_sconv):
    t = z.shape[0]
    n_batch, _, d_pool = state_pool.shape
    tm = _pick(n_prompt, (256, 128, 64))
    assert t % tm == 0
    n_tiles = n_prompt // tm
    scratch = lambda rows: [pltpu.VMEM((POOL_HIST_PAD + rows, d_pool), F32),
                            pltpu.VMEM((SCONV_HIST_PAD + rows, d_pool), F32)]
    consts = [w_grp, pool_scale.reshape(1, d_pool), w_sconv]
    const_specs = [pl.BlockSpec(w_grp.shape, lambda i: (0, 0, 0)),
                   pl.BlockSpec((1, d_pool), lambda i: (0, 0)),
                   pl.BlockSpec(w_sconv.shape, lambda i: (0, 0))]
    col = lambda c: pl.BlockSpec((tm, d_pool), lambda i: (i, c))
    hist = lambda rows, c: pl.BlockSpec(
        (rows, d_pool), lambda i: (jnp.maximum(i * (tm // rows) - 1, 0), c))
    y_p, utail_p = pl.pallas_call(
        functools.partial(_seq0_prompt_body, tm=tm, d_pool=d_pool, n_tiles=n_tiles),
        out_shape=[jax.ShapeDtypeStruct((t, 2 * d_pool), BF16),
                   jax.ShapeDtypeStruct((n_tiles * SCONV_HIST_PAD, d_pool), F32)],
        grid=(t // tm,),
        in_specs=[col(0), col(1), col(2), col(3), hist(POOL_HIST_PAD, 0),
                  hist(SCONV_HIST_PAD, 2), hist(SCONV_HIST_PAD, 3)] + const_specs,
        out_specs=[pl.BlockSpec((tm, 2 * d_pool), lambda i: (i, 0)),
                   pl.BlockSpec((SCONV_HIST_PAD, d_pool), lambda i: (jnp.minimum(i, n_tiles - 1), 0))],
        scratch_shapes=scratch(tm),
        compiler_params=_params("arbitrary"), name="seq0_prompt",
    )(z, z, z, z, z, z, z, *consts)

    c0 = n_prompt // CHUNK
    ph = jnp.pad(state_pool, ((0, 0), (POOL_HIST_PAD - state_pool.shape[1], 0), (0, 0)))
    uh = jnp.pad(state_sconv, ((0, 0), (SCONV_HIST_PAD - state_sconv.shape[1], 0), (0, 0)))
    scol = lambda c: pl.BlockSpec((CHUNK, d_pool), lambda i: (c0 + i, c))
    y, utail_s = pl.pallas_call(
        functools.partial(_seq0_sample_body, tm=CHUNK, d_pool=d_pool),
        out_shape=[jax.ShapeDtypeStruct((t, 2 * d_pool), BF16),
                   jax.ShapeDtypeStruct((n_batch * SCONV_HIST_PAD, d_pool), F32)],
        grid=(n_batch,),
        in_specs=[scol(0), scol(1), scol(2), scol(3),
                  pl.BlockSpec((None, POOL_HIST_PAD, d_pool), lambda i: (i, 0, 0)),
                  pl.BlockSpec((None, SCONV_HIST_PAD, d_pool), lambda i: (i, 0, 0))] + const_specs
        + [pl.BlockSpec(memory_space=pl.ANY)],
        out_specs=[pl.BlockSpec((CHUNK, 2 * d_pool), lambda i: (c0 + i, 0)),
                   pl.BlockSpec((SCONV_HIST_PAD, d_pool), lambda i: (i, 0))],
        scratch_shapes=scratch(CHUNK),
        input_output_aliases={9: 0},
        compiler_params=_params("arbitrary"), name="seq0_sample",
    )(z, z, z, z, ph, uh, *consts, y_p)
    return y, utail_p, utail_s


def _seq1_compute(uh, u_ref, wdw_ref, bdw_ref, lng_ref, lnb_ref, o_ref, uext, vbuf, *, tm, width):
    d = u_ref.shape[1]
    uext[pl.ds(0, DWCONV_HIST_PAD), :] = uh
    uext[pl.ds(DWCONV_HIST_PAD, tm), :] = u_ref[...]
    base = DWCONV_HIST_PAD - (width - 1)
    assert base >= 0 and width - 1 + base <= DWCONV_HIST_PAD
    cw = LANES
    rb = _pick(tm, (128, 64))

    def col_block(cb, carry):
        cols = pl.ds(pl.multiple_of(cb * cw, cw), cw)
        for r in range(tm // rb):
            acc = jnp.broadcast_to(bdw_ref[:, cols], (rb, cw))
            for res in range(SUBLANES):
                taps = [k for k in range(width) if (base + k) % SUBLANES == res]
                n = rb if res == 0 else rb + SUBLANES
                part = None
                for k in taps:
                    term = uext[pl.ds(r * rb + base + k - res, n), cols] * wdw_ref[pl.ds(k, 1), cols]
                    part = term if part is None else part + term
                if part is not None:
                    acc = acc + part[res:res + rb]
            vbuf[pl.ds(r * rb, rb), cols] = acc
        return carry

    lax.fori_loop(0, d // cw, col_block, 0)
    rb = CHUNK
    for r in range(tm // rb):
        rows = pl.ds(r * rb, rb)
        v = vbuf[rows, :]
        mu = jnp.mean(v, axis=-1, keepdims=True)
        vc = v - mu
        var = jnp.mean(vc * vc, axis=-1, keepdims=True)
        y = vc * lax.rsqrt(var + LN_EPS) * lng_ref[...] + lnb_ref[...]
        o_ref[rows, :] = (y * jax.nn.sigmoid(y)).astype(o_ref.dtype)


def _seq1_prompt_body(u_ref, uh_ref, wdw_ref, bdw_ref, lng_ref, lnb_ref, o_ref, uext, vbuf,
                      *, tm, width, n_tiles):
    i = pl.program_id(0)

    @pl.when(i < n_tiles)
    def _():
        live = (i > 0).astype(F32)
        _seq1_compute(uh_ref[...] * live, u_ref, wdw_ref, bdw_ref, lng_ref, lnb_ref, o_ref, uext, vbuf,
                      tm=tm, width=width)

    @pl.when(i >= n_tiles)
    def _():
        o_ref[...] = jnp.zeros_like(o_ref)


def _seq1_sample_body(u_ref, uh_ref, wdw_ref, bdw_ref, lng_ref, lnb_ref, prev_ref, o_ref, uext, vbuf,
                      *, tm, width):
    del prev_ref
    _seq1_compute(uh_ref[...], u_ref, wdw_ref, bdw_ref, lng_ref, lnb_ref, o_ref, uext, vbuf,
                  tm=tm, width=width)


def _seq1(u, n_prompt, state_dwconv, w_dw, b_dw, ln_g, ln_b):
    t, d = u.shape
    n_batch = state_dwconv.shape[0]
    width = w_dw.shape[0]
    tm = _pick(n_prompt, (256, 128, 64))
    assert t % tm == 0
    consts = [w_dw, b_dw.reshape(1, d), ln_g.reshape(1, d), ln_b.reshape(1, d)]
    const_specs = [pl.BlockSpec(w_dw.shape, lambda i: (0, 0))] + [
        pl.BlockSpec((1, d), lambda i: (0, 0)) for _ in range(3)]
    scratch = lambda rows: [pltpu.VMEM((DWCONV_HIST_PAD + rows, d), F32), pltpu.VMEM((rows, d), F32)]
    v_p = pl.pallas_call(
        functools.partial(_seq1_prompt_body, tm=tm, width=width, n_tiles=n_prompt // tm),
        out_shape=jax.ShapeDtypeStruct((t, d), BF16),
        grid=(t // tm,),
        in_specs=[pl.BlockSpec((tm, d), lambda i: (i, 0)),
                  pl.BlockSpec((DWCONV_HIST_PAD, d),
                               lambda i: (jnp.maximum(i * (tm // DWCONV_HIST_PAD) - 1, 0), 0))]
        + const_specs,
        out_specs=pl.BlockSpec((tm, d), lambda i: (i, 0)),
        scratch_shapes=scratch(tm),
        compiler_params=_params("arbitrary"), name="seq1_prompt",
    )(u, u, *consts)
    c0 = n_prompt // CHUNK
    uh = jnp.pad(state_dwconv, ((0, 0), (DWCONV_HIST_PAD - state_dwconv.shape[1], 0), (0, 0)))
    return pl.pallas_call(
        functools.partial(_seq1_sample_body, tm=CHUNK, width=width),
        out_shape=jax.ShapeDtypeStruct((t, d), BF16),
        grid=(n_batch,),
        in_specs=[pl.BlockSpec((CHUNK, d), lambda i: (c0 + i, 0)),
                  pl.BlockSpec((None, DWCONV_HIST_PAD, d), lambda i: (i, 0, 0))] + const_specs
        + [pl.BlockSpec(memory_space=pl.ANY)],
        out_specs=pl.BlockSpec((CHUNK, d), lambda i: (c0 + i, 0)),
        scratch_shapes=scratch(CHUNK),
        input_output_aliases={6: 0},
        compiler_params=_params("arbitrary"), name="seq1_sample",
    )(u, uh, *consts, v_p)


def _ffn_body(te_ref, nv_ref, h_ref, wg_ref, wu_ref, wd_ref, *refs, n_chunk, n_cast):
    del te_ref
    cast_in, o_ref, cast_out = refs[:n_cast], refs[n_cast], refs[n_cast + 1:]
    i = pl.program_id(0)
    j = pl.program_id(1)

    @pl.when(j == 0)
    def _():
        o_ref[...] = jnp.zeros_like(o_ref)

    @pl.when(i < nv_ref[0])
    def _():
        for src, dst in zip(cast_in, cast_out):
            dst[...] = src[...].astype(dst.dtype)
        h = h_ref[...]
        g = jnp.dot(h, wg_ref[...], preferred_element_type=F32)
        u = jnp.dot(h, wu_ref[...], preferred_element_type=F32)
        a = (g * jax.nn.sigmoid(g) * u).astype(BF16)
        d = o_ref.shape[1]
        for c in range(d // n_chunk):
            cols = pl.ds(c * n_chunk, n_chunk)
            o_ref[:, cols] += jnp.dot(a, wd_ref[:, cols], preferred_element_type=F32)


def _ffn(h, w_gate, w_up, w_down, tile_set, n_valid, *, tm, tf, cast=()):
    p, d = h.shape
    f = w_gate.shape[2]
    n_tiles, n_f = p // tm, f // tf
    cast_specs, cast_shapes = _cast_blocks(cast, n_tiles * n_f, lambda i, j, te, nv: i * n_f + j)

    def row_map(i, j, te, nv):
        return (jnp.minimum(i, nv[0] - 1), 0)

    def _ij(i, j, te, nv):
        live = i < nv[0]
        return te[jnp.minimum(i, nv[0] - 1)], jnp.where(live, j, n_f - 1)

    def up_map(i, j, te, nv):
        e, jj = _ij(i, j, te, nv)
        return (e, 0, jj)

    def down_map(i, j, te, nv):
        e, jj = _ij(i, j, te, nv)
        return (e, jj, 0)

    grid_spec = pltpu.PrefetchScalarGridSpec(
        num_scalar_prefetch=2, grid=(n_tiles, n_f),
        in_specs=[pl.BlockSpec((tm, d), row_map),
                  pl.BlockSpec((None, d, tf), up_map),
                  pl.BlockSpec((None, d, tf), up_map),
                  pl.BlockSpec((None, tf, d), down_map)] + cast_specs,
        out_specs=[pl.BlockSpec((tm, d), lambda i, j, te, nv: (i, 0))] + cast_specs)
    return pl.pallas_call(
        functools.partial(_ffn_body, n_chunk=_pick(d, (1024, 512, 256, 128)), n_cast=len(cast)),
        out_shape=[jax.ShapeDtypeStruct((p, d), F32)] + cast_shapes, grid_spec=grid_spec,
        compiler_params=_params("arbitrary", "arbitrary"), name="ffn",
    )(tile_set, n_valid, h, w_gate, w_up, w_down, *cast)


def _row_copy(src_hbm, row, dst, r, sem):
    return pltpu.make_async_copy(src_hbm.at[pl.ds(row, 1), :], dst.at[pl.ds(r, 1), :], sem)


def _ring_step(n_steps, start_rows, wait_rows):
    i = pl.program_id(0)
    slot = lax.rem(i, RING_SLOTS)

    @pl.when(i == 0)
    def _():
        start_rows(False, 0)

    @pl.when(i + 1 < n_steps)
    def _():
        start_rows(True, 1 - slot)

    wait_rows(slot)
    return slot


def _gather_body(src_ref, src_next_ref, h_hbm, o_ref, buf, sem, *, rows, n_steps):
    def start_rows(is_next, slot):
        idx = src_next_ref if is_next else src_ref

        def start(q, c):
            for k in range(DMA_PRIORITIES):
                r = DMA_PRIORITIES * q + k
                _row_copy(h_hbm, idx[0, 0, r], buf.at[slot], r, sem.at[slot]).start(priority=k)
            return c

        lax.fori_loop(0, rows // DMA_PRIORITIES, start, 0, unroll=DMA_UNROLL // DMA_PRIORITIES)

    def wait_rows(slot):
        def wait(r, c):
            _row_copy(h_hbm, 0, buf.at[slot], r, sem.at[slot]).wait()
            return c

        lax.fori_loop(0, rows, wait, 0, unroll=DMA_UNROLL)

    slot = _ring_step(n_steps, start_rows, wait_rows)
    o_ref[...] = buf[slot].astype(o_ref.dtype)


def _gather_rows(h, src, *, rows):
    d = h.shape[1]
    p = src.shape[0]
    n = p // rows
    src = src.reshape(n, 1, rows)
    idx_spec = lambda step: pl.BlockSpec((1, 1, rows), lambda i: (jnp.minimum(i + step, n - 1), 0, 0),
                                         memory_space=pltpu.SMEM)
    return pl.pallas_call(
        functools.partial(_gather_body, rows=rows, n_steps=n),
        out_shape=jax.ShapeDtypeStruct((p, d), BF16),
        grid=(n,),
        in_specs=[idx_spec(0), idx_spec(1), pl.BlockSpec(memory_space=pl.ANY)],
        out_specs=pl.BlockSpec((rows, d), lambda i: (i, 0)),
        scratch_shapes=[pltpu.VMEM((RING_SLOTS, rows, d), F32), pltpu.SemaphoreType.DMA((RING_SLOTS,))],
        compiler_params=_params("arbitrary"), name="gather_rows",
    )(src, src, h)


def _combine_body(pos_ref, pos_next_ref, cw_ref, x_ref, gate_ref, g_ref, y_hbm, op_ref, os_ref, buf, sem,
                  *, rows, n_steps, n_prompt_tiles, n_prompt_chunks):
    i = pl.program_id(0)

    def start_rows(is_next, slot):
        idx = pos_next_ref if is_next else pos_ref

        def start(r, c):
            for k in range(TOP_K):
                _row_copy(y_hbm, idx[0, 0, TOP_K * r + k], buf.at[slot, k], r,
                          sem.at[slot]).start(priority=k % DMA_PRIORITIES)
            return c

        lax.fori_loop(0, rows, start, 0, unroll=DMA_UNROLL)

    def wait_rows(slot):
        def wait(r, c):
            for k in range(TOP_K):
                _row_copy(y_hbm, 0, buf.at[slot, k], r, sem.at[slot]).wait()
            return c

        lax.fori_loop(0, rows, wait, 0, unroll=DMA_UNROLL)

    slot = _ring_step(n_steps, start_rows, wait_rows)

    def emit(o_ref):
        for c in range(rows // CHUNK):
            rs = pl.ds(c * CHUNK, CHUNK)
            b = _chunk_batch(i * (rows // CHUNK) + c, n_prompt_chunks)
            cw = cw_ref[rs, :]
            f = cw[:, 0:1] * buf[slot, 0, rs, :] + cw[:, 1:2] * buf[slot, 1, rs, :]
            xv = x_ref[rs, :] + gate_ref[pl.ds(b, 1), :] * f
            ms = jnp.mean(xv * xv, axis=-1, keepdims=True)
            o_ref[rs, :] = xv * lax.rsqrt(ms + RMS_EPS) * g_ref[...]

    @pl.when(i < n_prompt_tiles)
    def _():
        emit(op_ref)

    @pl.when(i >= n_prompt_tiles)
    def _():
        emit(os_ref)


def _combine(x, y_sorted, pos, cw, mod, gate, g_final, n_prompt):
    t, d = x.shape
    rows = _pick(n_prompt, (256, 128, 64))
    n = t // rows
    npt = n_prompt // rows
    layer, gate_col = gate
    pos = pos.reshape(n, 1, TOP_K * rows)
    idx_spec = lambda step: pl.BlockSpec((1, 1, TOP_K * rows),
                                         lambda i: (jnp.minimum(i + step, n - 1), 0, 0),
                                         memory_space=pltpu.SMEM)
    return pl.pallas_call(
        functools.partial(_combine_body, rows=rows, n_steps=n, n_prompt_tiles=npt,
                          n_prompt_chunks=n_prompt // CHUNK),
        out_shape=[jax.ShapeDtypeStruct((n_prompt, d), F32),
                   jax.ShapeDtypeStruct((t - n_prompt, d), F32)],
        grid=(n,),
        in_specs=[idx_spec(0), idx_spec(1),
                  pl.BlockSpec((rows, ROUTE_LANES), lambda i: (i, 0)),
                  pl.BlockSpec((rows, d), lambda i: (i, 0)),
                  pl.BlockSpec((None, MOD_ROWS, d), lambda i: (layer, 0, gate_col)),
                  pl.BlockSpec((1, d), lambda i: (0, 0)),
                  pl.BlockSpec(memory_space=pl.ANY)],
        out_specs=[pl.BlockSpec((rows, d), lambda i: (jnp.minimum(i, npt - 1), 0)),
                   pl.BlockSpec((rows, d), lambda i: (jnp.maximum(i - npt, 0), 0))],
        scratch_shapes=[pltpu.VMEM((RING_SLOTS, TOP_K, rows, d), F32),
                        pltpu.SemaphoreType.DMA((RING_SLOTS,))],
        compiler_params=_params("arbitrary"), name="combine",
    )(pos, pos, cw, x, mod, g_final.reshape(1, d), y_sorted)


def _route_plan(e_idx, n_experts, tm, n_tiles):
    t = e_idx.shape[0]
    flat_e = e_idx.reshape(-1)
    onehot = (flat_e[:, None] == jnp.arange(n_experts)[None, :]).astype(jnp.int32)
    csum = jnp.cumsum(onehot, axis=0)
    rank = jnp.take_along_axis(csum, flat_e[:, None], axis=1)[:, 0] - 1
    counts = csum[-1]
    tiles_per = (counts + tm - 1) // tm
    tile_end = jnp.cumsum(tiles_per)
    row_start = (tile_end - tiles_per) * tm
    pos = row_start[flat_e] + rank
    src = jnp.zeros((n_tiles * tm,), jnp.int32).at[pos].set(jnp.arange(TOP_K * t, dtype=jnp.int32) // TOP_K)
    before = (tile_end[None, :] <= jnp.arange(n_tiles)[:, None]).astype(jnp.int32)
    tile_set = jnp.minimum(jnp.sum(before, axis=1), n_experts - 1).astype(jnp.int32)
    n_valid = tile_end[-1:].astype(jnp.int32)
    return pos.astype(jnp.int32).reshape(t, TOP_K), src, tile_set, n_valid


def kernel(x_prompt, x_sample, state_pool, state_shortconv, state_dwconv, c_prompt, c_sample, w_ada, b_ada, g_mix, g_ffn, g_final, w_in_ab, w_pool_grp, pool_scale, w_sconv, w_out_ab, w_ffn_gate, w_ffn_up, w_ffn_down, w_pw1, b_pw1, w_dw, b_dw, ln_g, ln_b, w_pw2, b_pw2, w_router, w_exp_gate, w_exp_up, w_exp_down):
    n_pb, seq, d = x_prompt.shape
    n_sb, dec_seq, _ = x_sample.shape
    assert n_pb == 1 and dec_seq == CHUNK and seq % CHUNK == 0
    assert 1 + n_sb <= MOD_ROWS and w_ada.shape[0] == 2
    n_prompt = seq
    t = n_prompt + n_sb * dec_seq
    npc = n_prompt // CHUNK
    d_pool = state_pool.shape[-1]
    n_experts = w_router.shape[-1]

    x = (x_prompt.reshape(n_prompt, d), x_sample.reshape(n_sb * dec_seq, d))
    c_all = jnp.concatenate([c_prompt, c_sample, jnp.zeros((MOD_ROWS - 1 - n_sb, d), F32)], axis=0)
    mod = _mod_tables(c_all, w_ada, b_ada)
    bf = lambda w: w.astype(BF16)

    (h,) = _addnorm(x, g_mix[0], npc, mod, shift=(0, 0), scale=(0, 1))
    z, wg0, wu0, wd0, w_out_b, w_pw1_b, w_pw2_b = _matmul(
        h, bf(w_in_ab[0]),
        cast=(w_ffn_gate[0], w_ffn_up[0], w_ffn_down[0], w_out_ab[0], w_pw1[0], w_pw2[0]))
    y_cat, utail_p, utail_s = _seq0(z, n_prompt, state_pool[0], state_shortconv[0],
                                    bf(w_pool_grp[0]), pool_scale[0], w_sconv[0])
    y = _matmul(y_cat, w_out_b)
    x, h = _addnorm(x, g_ffn[0], npc, mod, y=y, gate=(0, 2), shift=(0, 3), scale=(0, 4))
    tm_d = _pick(t, (512, 256, 128, 64))
    n_td = t // tm_d
    d_ffe = w_exp_gate.shape[3]
    f, wg_e, wu_e = _ffn(h, wg0[None], wu0[None], wd0[None],
                         jnp.zeros((n_td,), jnp.int32), jnp.full((1,), n_td, jnp.int32),
                         tm=tm_d, tf=_pick(w_ffn_gate.shape[2], (256, 128)),
                         cast=(w_exp_gate[0].reshape(n_experts * d, d_ffe),
                               w_exp_up[0].reshape(n_experts * d, d_ffe)))
    wg_e = wg_e.reshape(n_experts, d, d_ffe)
    wu_e = wu_e.reshape(n_experts, d, d_ffe)

    x, h = _addnorm(x, g_mix[1], npc, mod, y=f, gate=(0, 5), shift=(1, 0), scale=(1, 1))
    u = _matmul_glu(h, w_pw1_b, b_pw1[0])
    v = _seq1(u, n_prompt, state_dwconv[0], w_dw[0], b_dw[0], ln_g[0], ln_b[0])
    y = _matmul(v, w_pw2_b, b_pw2[0])
    x, h32, cw, e_idx = _addnorm(x, g_ffn[1], npc, mod, y=y, gate=(1, 2), shift=(1, 3), scale=(1, 4),
                                 h_dtypes=(F32,), w_router=w_router[0])
    tm_e = _pick(TOP_K * t, (512, 256, 128, 64))
    n_te = TOP_K * t // tm_e + n_experts
    pos, src, tile_set, n_valid = _route_plan(e_idx[:, :TOP_K], n_experts, tm_e, n_te)
    h_sorted = _gather_rows(h32, src, rows=tm_e)
    (y_sorted,) = _ffn(h_sorted, wg_e, wu_e, bf(w_exp_down[0]), tile_set, n_valid,
                       tm=tm_e, tf=_pick(d_ffe, (512, 256, 128)))
    y_p, y_s = _combine(x, y_sorted, pos, cw, mod, (1, 5), g_final, n_prompt)

    n_tiles0 = utail_p.shape[0] // SCONV_HIST_PAD
    sconv_keep = state_shortconv.shape[2]
    dw_keep = state_dwconv.shape[2]
    zs = z[n_prompt:].reshape(n_sb, dec_seq, -1)
    us = u[n_prompt:].reshape(n_sb, dec_seq, -1)
    new_pool_p = z[n_prompt - POOL_HIST:n_prompt, :d_pool][None, None]
    new_pool_s = zs[:, dec_seq - POOL_HIST:, :d_pool][None]
    new_sconv_p = utail_p[(n_tiles0 - 1) * SCONV_HIST_PAD:][SCONV_HIST_PAD - sconv_keep:][None, None]
    new_sconv_s = utail_s.reshape(n_sb, SCONV_HIST_PAD, -1)[:, SCONV_HIST_PAD - sconv_keep:][None]
    new_dw_p = u[n_prompt - dw_keep:n_prompt][None, None]
    new_dw_s = us[:, dec_seq - dw_keep:][None]
    return (y_p.reshape(1, n_prompt, d), y_s.reshape(n_sb, dec_seq, d),
            new_pool_p, new_sconv_p, new_dw_p, new_pool_s, new_sconv_s, new_dw_s)
```

```python
import functools

import jax
import jax.numpy as jnp
from jax import lax
from jax.experimental import pallas as pl
from jax.experimental.pallas import tpu as pltpu

F32 = jnp.float32
BF16 = jnp.bfloat16

CHUNK = 64
POOL_WINDOWS = (2, 4, 8, 16)
POOL_HIST = max(POOL_WINDOWS) - 1
POOL_HIST_PAD = 16
SCONV_HIST_PAD = 8
DWCONV_HIST_PAD = 32
PAST_LEN = 4096
TOP_K = 2
RMS_EPS = 1e-6
LN_EPS = 1e-5
MOD_ROWS = 16
RING_SLOTS = 2
DMA_UNROLL = 8
SUBLANES = 8
LANES = 128
ROUTE_LANES = LANES
V7X_VMEM_BYTES = 64 * 1024 * 1024
VMEM_LIMIT = V7X_VMEM_BYTES - 6 * 1024 * 1024


def _params(*sem):
    return pltpu.CompilerParams(dimension_semantics=sem, vmem_limit_bytes=VMEM_LIMIT)


def _pick(n, prefs):
    for p in prefs:
        if n % p == 0:
            return p
    return n


def _mod_body(c_ref, w_ref, b_ref, o_ref):
    c = c_ref[...]
    sc = (c * jax.nn.sigmoid(c)).astype(BF16)
    o_ref[...] = jnp.dot(sc, w_ref[...].astype(BF16), preferred_element_type=F32) + b_ref[...]


def _mod_tables(c_all, w_ada, b_ada):
    depth, d, n = w_ada.shape
    tn = _pick(n, (512, 256, 128))
    return pl.pallas_call(
        _mod_body,
        out_shape=jax.ShapeDtypeStruct((depth, MOD_ROWS, n), F32),
        grid=(depth, n // tn),
        in_specs=[
            pl.BlockSpec((MOD_ROWS, d), lambda l, j: (0, 0)),
            pl.BlockSpec((None, d, tn), lambda l, j: (l, 0, j)),
            pl.BlockSpec((None, 1, tn), lambda l, j: (l, 0, j)),
        ],
        out_specs=pl.BlockSpec((None, MOD_ROWS, tn), lambda l, j: (l, 0, j)),
        compiler_params=_params("arbitrary", "arbitrary"),
        name="mod_tables",
    )(c_all, w_ada, b_ada.reshape(depth, 1, n))


def _chunk_batch(chunk_idx, n_prompt_chunks):
    return jnp.maximum(chunk_idx - (n_prompt_chunks - 1), 0)


HI16 = 0xFFFF0000


def _pack_bf16_pairs(v):
    half = v.shape[1] // 2
    bits = lax.bitcast_convert_type(v.astype(BF16).astype(F32), jnp.uint32)
    return (bits[:, :half] >> 16) | (bits[:, half:] & jnp.uint32(HI16))


def _unpack_bf16_pairs(w):
    lo = lax.bitcast_convert_type(w << 16, F32).astype(BF16)
    hi = lax.bitcast_convert_type(w & jnp.uint32(HI16), F32).astype(BF16)
    return lo, hi


def _top2(logits, n_experts):
    lane = lax.broadcasted_iota(jnp.int32, logits.shape, 1)
    lane_f = lane.astype(F32)
    neg = jnp.float32(-jnp.inf)
    lg = jnp.where(lane < n_experts, logits, neg)
    m1 = jnp.max(lg, axis=-1, keepdims=True)
    i1 = jnp.min(jnp.where(lg == m1, lane_f, float(ROUTE_LANES)), axis=-1, keepdims=True)
    lg2 = jnp.where(lane_f == i1, neg, lg)
    m2 = jnp.max(lg2, axis=-1, keepdims=True)
    i2 = jnp.min(jnp.where(lg2 == m2, lane_f, float(ROUTE_LANES)), axis=-1, keepdims=True)
    ex = jnp.exp(m2 - m1)
    den = 1.0 + ex
    weights = jnp.where(lane == 0, 1.0 / den, jnp.where(lane == 1, ex / den, 0.0))
    experts = jnp.where(lane == 0, i1, jnp.where(lane == 1, i2, 0.0)).astype(jnp.int32)
    return weights, experts


def _addnorm_body(*refs, tm, n_prompt_chunks, n_prompt_tiles, split_x, has_add, has_mod, n_experts):
    refs = list(refs)
    x_ref = refs.pop(0)
    xs_ref = refs.pop(0) if split_x else None
    if has_add:
        y_ref = refs.pop(0)
        gate_ref = refs.pop(0)
    g_ref = refs.pop(0)
    if has_mod:
        shift_ref = refs.pop(0)
        scale_ref = refs.pop(0)
    if n_experts:
        wr_ref = refs.pop(0)
    if has_add:
        xo_ref = refs.pop(0)
    if n_experts:
        re_ref = refs.pop()
        rw_ref = refs.pop()
    h_refs = refs
    i = pl.program_id(0)
    for c in range(tm // CHUNK):
        rows = pl.ds(c * CHUNK, CHUNK)
        b = _chunk_batch(i * (tm // CHUNK) + c, n_prompt_chunks)
        xv = x_ref[rows, :]
        if split_x:
            xv = jnp.where(i < n_prompt_tiles, xv, xs_ref[rows, :])
        if has_add:
            xv = xv + gate_ref[pl.ds(b, 1), :] * y_ref[rows, :]
            xo_ref[rows, :] = xv
        ms = jnp.mean(xv * xv, axis=-1, keepdims=True)
        hv = xv * lax.rsqrt(ms + RMS_EPS) * g_ref[...]
        if has_mod:
            hv = hv * (1.0 + scale_ref[pl.ds(b, 1), :]) + shift_ref[pl.ds(b, 1), :]
        for h_ref in h_refs:
            if h_ref.dtype == jnp.uint32:
                h_ref[rows, :] = _pack_bf16_pairs(hv)
            else:
                h_ref[rows, :] = hv.astype(h_ref.dtype)
        if n_experts:
            logits = jnp.dot(hv, wr_ref[...], preferred_element_type=F32,
                             precision=lax.Precision.HIGHEST)
            rw_ref[rows, :], re_ref[rows, :] = _top2(logits, n_experts)


def _addnorm(x, g, n_prompt_chunks, mod, *, y=None, gate=None, shift=None, scale=None,
             h_dtypes=(BF16,), w_router=None):
    split_x = isinstance(x, tuple)
    n_prompt = n_prompt_chunks * CHUNK
    t = sum(p.shape[0] for p in x) if split_x else x.shape[0]
    d = g.shape[0]
    tm = _pick(n_prompt, (256, 128, 64))
    assert t % tm == 0
    npt = n_prompt // tm
    has_add = y is not None
    has_mod = shift is not None
    n_experts = 0 if w_router is None else w_router.shape[1]
    row = pl.BlockSpec((tm, d), lambda i: (i, 0))

    def tab(where):
        layer, col = where
        return pl.BlockSpec((None, MOD_ROWS, d), lambda i: (layer, 0, col))

    if split_x:
        args = list(x)
        specs = [pl.BlockSpec((tm, d), lambda i: (jnp.minimum(i, npt - 1), 0)),
                 pl.BlockSpec((tm, d), lambda i: (jnp.maximum(i - npt, 0), 0))]
    else:
        args, specs = [x], [row]
    if has_add:
        args += [y, mod]
        specs += [row, tab(gate)]
    args.append(g.reshape(1, d))
    specs.append(pl.BlockSpec((1, d), lambda i: (0, 0)))
    if has_mod:
        args += [mod, mod]
        specs += [tab(shift), tab(scale)]
    if n_experts:
        args.append(jnp.pad(w_router, ((0, 0), (0, ROUTE_LANES - n_experts))))
        specs.append(pl.BlockSpec((d, ROUTE_LANES), lambda i: (0, 0)))
    out_shape, out_specs = [], []
    if has_add:
        out_shape.append(jax.ShapeDtypeStruct((t, d), F32))
        out_specs.append(row)
    for dt in h_dtypes:
        width = d // 2 if dt == jnp.uint32 else d
        out_shape.append(jax.ShapeDtypeStruct((t, width), dt))
        out_specs.append(pl.BlockSpec((tm, width), lambda i: (i, 0)))
    if n_experts:
        for dt in (F32, jnp.int32):
            out_shape.append(jax.ShapeDtypeStruct((t, ROUTE_LANES), dt))
            out_specs.append(pl.BlockSpec((tm, ROUTE_LANES), lambda i: (i, 0)))
    body = functools.partial(_addnorm_body, tm=tm, n_prompt_chunks=n_prompt_chunks, n_prompt_tiles=npt,
                             split_x=split_x, has_add=has_add, has_mod=has_mod, n_experts=n_experts)
    return pl.pallas_call(
        body, out_shape=out_shape, grid=(t // tm,), in_specs=specs, out_specs=out_specs,
        compiler_params=_params("arbitrary"), name="addnorm",
    )(*args)


def _cast_blocks(cast, n_steps, step_of):
    specs, shapes = [], []
    for w in cast:
        rows = w.shape[0]
        rb = next(r for r in (16, 32, 64, 128, 256, 512, 1024, 2048, 4096)
                  if rows % r == 0 and rows // r <= n_steps)
        nblk = rows // rb
        specs.append(pl.BlockSpec(
            (rb, w.shape[1]), lambda *ids, nblk=nblk: (jnp.minimum(step_of(*ids), nblk - 1), 0)))
        shapes.append(jax.ShapeDtypeStruct(w.shape, BF16))
    return specs, shapes


def _mm_body(a_ref, b_ref, *refs, has_bias, n_cast):
    refs = list(refs)
    bias_ref = refs.pop(0) if has_bias else None
    cast_in, o_ref, cast_out = refs[:n_cast], refs[n_cast], refs[n_cast + 1:]
    for src, dst in zip(cast_in, cast_out):
        dst[...] = src[...].astype(dst.dtype)
    acc = jnp.dot(a_ref[...], b_ref[...], preferred_element_type=F32)
    if has_bias:
        acc = acc + bias_ref[...]
    o_ref[...] = acc


def _matmul(a, b, bias=None, cast=()):
    t, k = a.shape
    n = b.shape[1]
    tm = _pick(t, (768, 512, 256, 128, 64))
    tn = _pick(n, (1024, 512, 256, 128))
    n_j = n // tn
    args = [a, b]
    specs = [pl.BlockSpec((tm, k), lambda i, j: (i, 0)), pl.BlockSpec((k, tn), lambda i, j: (0, j))]
    if bias is not None:
        args.append(bias.reshape(1, n))
        specs.append(pl.BlockSpec((1, tn), lambda i, j: (0, j)))
    cast_specs, cast_shapes = _cast_blocks(cast, (t // tm) * n_j, lambda i, j: i * n_j + j)
    out = pl.pallas_call(
        functools.partial(_mm_body, has_bias=bias is not None, n_cast=len(cast)),
        out_shape=[jax.ShapeDtypeStruct((t, n), F32)] + cast_shapes,
        grid=(t // tm, n_j), in_specs=specs + cast_specs,
        out_specs=[pl.BlockSpec((tm, tn), lambda i, j: (i, j))] + cast_specs,
        compiler_params=_params("arbitrary", "arbitrary"), name="matmul",
    )(*args, *cast)
    return out if cast else out[0]


def _glu_body(a_ref, ba_ref, bg_ref, bias_a_ref, bias_g_ref, o_ref):
    a = a_ref[...]
    za = jnp.dot(a, ba_ref[...], preferred_element_type=F32) + bias_a_ref[...]
    zg = jnp.dot(a, bg_ref[...], preferred_element_type=F32) + bias_g_ref[...]
    o_ref[...] = za * jax.nn.sigmoid(zg)


def _matmul_glu(a, b, bias):
    t, k = a.shape
    n = b.shape[1] // 2
    tm = _pick(t, (768, 512, 256, 128, 64))
    tn = _pick(n, (512, 256, 128))
    nb = n // tn
    bias = bias.reshape(1, 2 * n)
    return pl.pallas_call(
        _glu_body,
        out_shape=jax.ShapeDtypeStruct((t, n), F32),
        grid=(t // tm, nb),
        in_specs=[
            pl.BlockSpec((tm, k), lambda i, j: (i, 0)),
            pl.BlockSpec((k, tn), lambda i, j: (0, j)),
            pl.BlockSpec((k, tn), lambda i, j: (0, j + nb)),
            pl.BlockSpec((1, tn), lambda i, j: (0, j)),
            pl.BlockSpec((1, tn), lambda i, j: (0, j + nb)),
        ],
        out_specs=pl.BlockSpec((tm, tn), lambda i, j: (i, j)),
        compiler_params=_params("arbitrary", "arbitrary"), name="matmul_glu",
    )(a, b, b, bias, bias)


def _seq0_compute(pos_base, ph, uh, p_ref, bg_ref, cg_ref, v_ref, wgrp_ref, scale_ref, wsc_ref,
                  o_ref, utail_ref, pext, uext, *, tm, d_pool):
    n_groups = len(POOL_WINDOWS)
    gw = d_pool // n_groups
    pext[pl.ds(0, POOL_HIST_PAD), :] = ph
    pext[pl.ds(POOL_HIST_PAD, tm), :] = p_ref[...]
    u = cg_ref[...] * v_ref[...]
    uext[pl.ds(0, SCONV_HIST_PAD), :] = uh
    uext[pl.ds(SCONV_HIST_PAD, tm), :] = u
    utail_ref[...] = u[tm - SCONV_HIST_PAD:, :]

    pos = pos_base + lax.broadcasted_iota(jnp.int32, (tm, 1), 0)
    for gi, w in enumerate(POOL_WINDOWS):
        cols = pl.ds(gi * gw, gw)
        cur = pext[pl.ds(POOL_HIST_PAD, tm), cols]
        s = cur
        for k in range(1, w):
            s = s + pext[pl.ds(POOL_HIST_PAD - k, tm), cols]
        cnt = jnp.minimum(w, pos + 1).astype(F32)
        dlt = (s / cnt - cur).astype(BF16)
        ya = jnp.dot(dlt, wgrp_ref[gi], preferred_element_type=F32) * scale_ref[:, cols]
        o_ref[:, cols] = ya.astype(o_ref.dtype)

    conv = (uext[pl.ds(SCONV_HIST_PAD - 2, tm), :] * wsc_ref[pl.ds(0, 1), :]
            + uext[pl.ds(SCONV_HIST_PAD - 1, tm), :] * wsc_ref[pl.ds(1, 1), :]
            + uext[pl.ds(SCONV_HIST_PAD, tm), :] * wsc_ref[pl.ds(2, 1), :])
    o_ref[:, pl.ds(d_pool, d_pool)] = (bg_ref[...] * conv).astype(o_ref.dtype)


def _seq0_prompt_body(p_ref, bg_ref, cg_ref, v_ref, ph_ref, cgh_ref, vh_ref, wgrp_ref, scale_ref,
                      wsc_ref, o_ref, utail_ref, pext, uext, *, tm, d_pool, n_tiles):
    i = pl.program_id(0)

    @pl.when(i < n_tiles)
    def _():
        live = (i > 0).astype(F32)
        ph = ph_ref[...] * live
        uh = cgh_ref[...] * vh_ref[...] * live
        _seq0_compute(i * tm, ph, uh, p_ref, bg_ref, cg_ref, v_ref, wgrp_ref, scale_ref, wsc_ref,
                      o_ref, utail_ref, pext, uext, tm=tm, d_pool=d_pool)

    @pl.when(i >= n_tiles)
    def _():
        o_ref[...] = jnp.zeros_like(o_ref)


def _seq0_sample_body(p_ref, bg_ref, cg_ref, v_ref, ph_ref, uh_ref, wgrp_ref, scale_ref, wsc_ref,
                      prev_ref, o_ref, utail_ref, pext, uext, *, tm, d_pool):
    del prev_ref
    _seq0_compute(PAST_LEN, ph_ref[...], uh_ref[...], p_ref, bg_ref, cg_ref, v_ref, wgrp_ref,
                  scale_ref, wsc_ref, o_ref, utail_ref, pext, uext, tm=tm, d_pool=d_pool)


def _seq0(z, n_prompt, state_pool, state_sconv, w_grp, pool_scale, w_sconv):
    t = z.shape[0]
    n_batch, _, d_pool = state_pool.shape
    tm = _pick(n_prompt, (256, 128, 64))
    assert t % tm == 0
    n_tiles = n_prompt // tm
    scratch = lambda rows: [pltpu.VMEM((POOL_HIST_PAD + rows, d_pool), F32),
                            pltpu.VMEM((SCONV_HIST_PAD + rows, d_pool), F32)]
    consts = [w_grp, pool_scale.reshape(1, d_pool), w_sconv]
    const_specs = [pl.BlockSpec(w_grp.shape, lambda i: (0, 0, 0)),
                   pl.BlockSpec((1, d_pool), lambda i: (0, 0)),
                   pl.BlockSpec(w_sconv.shape, lambda i: (0, 0))]
    col = lambda c: pl.BlockSpec((tm, d_pool), lambda i: (i, c))
    hist = lambda rows, c: pl.BlockSpec(
        (rows, d_pool), lambda i: (jnp.maximum(i * (tm // rows) - 1, 0), c))
    y_p, utail_p = pl.pallas_call(
        functools.partial(_seq0_prompt_body, tm=tm, d_pool=d_pool, n_tiles=n_tiles),
        out_shape=[jax.ShapeDtypeStruct((t, 2 * d_pool), BF16),
                   jax.ShapeDtypeStruct((n_tiles * SCONV_HIST_PAD, d_pool), F32)],
        grid=(t // tm,),
        in_specs=[col(0), col(1), col(2), col(3), hist(POOL_HIST_PAD, 0),
                  hist(SCONV_HIST_PAD, 2), hist(SCONV_HIST_PAD, 3)] + const_specs,
        out_specs=[pl.BlockSpec((tm, 2 * d_pool), lambda i: (i, 0)),
                   pl.BlockSpec((SCONV_HIST_PAD, d_pool), lambda i: (jnp.minimum(i, n_tiles - 1), 0))],
        scratch_shapes=scratch(tm),
        compiler_params=_params("arbitrary"), name="seq0_prompt",
    )(z, z, z, z, z, z, z, *consts)

    c0 = n_prompt // CHUNK
    ph = jnp.pad(state_pool, ((0, 0), (POOL_HIST_PAD - state_pool.shape[1], 0), (0, 0)))
    uh = jnp.pad(state_sconv, ((0, 0), (SCONV_HIST_PAD - state_sconv.shape[1], 0), (0, 0)))
    scol = lambda c: pl.BlockSpec((CHUNK, d_pool), lambda i: (c0 + i, c))
    y, utail_s = pl.pallas_call(
        functools.partial(_seq0_sample_body, tm=CHUNK, d_pool=d_pool),
        out_shape=[jax.ShapeDtypeStruct((t, 2 * d_pool), BF16),
                   jax.ShapeDtypeStruct((n_batch * SCONV_HIST_PAD, d_pool), F32)],
        grid=(n_batch,),
        in_specs=[scol(0), scol(1), scol(2), scol(3),
                  pl.BlockSpec((None, POOL_HIST_PAD, d_pool), lambda i: (i, 0, 0)),
                  pl.BlockSpec((None, SCONV_HIST_PAD, d_pool), lambda i: (i, 0, 0))] + const_specs
        + [pl.BlockSpec(memory_space=pl.ANY)],
        out_specs=[pl.BlockSpec((CHUNK, 2 * d_pool), lambda i: (c0 + i, 0)),
                   pl.BlockSpec((SCONV_HIST_PAD, d_pool), lambda i: (i, 0))],
        scratch_shapes=scratch(CHUNK),
        input_output_aliases={9: 0},
        compiler_params=_params("arbitrary"), name="seq0_sample",
    )(z, z, z, z, ph, uh, *consts, y_p)
    return y, utail_p, utail_s


def _seq1_compute(uh, u_ref, wdw_ref, bdw_ref, lng_ref, lnb_ref, o_ref, uext, vbuf, *, tm, width):
    d = u_ref.shape[1]
    uext[pl.ds(0, DWCONV_HIST_PAD), :] = uh
    uext[pl.ds(DWCONV_HIST_PAD, tm), :] = u_ref[...]
    base = DWCONV_HIST_PAD - (width - 1)
    assert base >= 0 and width - 1 + base <= DWCONV_HIST_PAD
    cw = LANES
    rb = _pick(tm, (128, 64))

    def col_block(cb, carry):
        cols = pl.ds(pl.multiple_of(cb * cw, cw), cw)
        for r in range(tm // rb):
            acc = jnp.broadcast_to(bdw_ref[:, cols], (rb, cw))
            for res in range(SUBLANES):
                taps = [k for k in range(width) if (base + k) % SUBLANES == res]
                n = rb if res == 0 else rb + SUBLANES
                part = None
                for k in taps:
                    term = uext[pl.ds(r * rb + base + k - res, n), cols] * wdw_ref[pl.ds(k, 1), cols]
                    part = term if part is None else part + term
                if part is not None:
                    acc = acc + part[res:res + rb]
            vbuf[pl.ds(r * rb, rb), cols] = acc
        return carry

    lax.fori_loop(0, d // cw, col_block, 0)
    rb = CHUNK
    for r in range(tm // rb):
        rows = pl.ds(r * rb, rb)
        v = vbuf[rows, :]
        mu = jnp.mean(v, axis=-1, keepdims=True)
        vc = v - mu
        var = jnp.mean(vc * vc, axis=-1, keepdims=True)
        y = vc * lax.rsqrt(var + LN_EPS) * lng_ref[...] + lnb_ref[...]
        o_ref[rows, :] = (y * jax.nn.sigmoid(y)).astype(o_ref.dtype)


def _seq1_prompt_body(u_ref, uh_ref, wdw_ref, bdw_ref, lng_ref, lnb_ref, o_ref, uext, vbuf,
                      *, tm, width, n_tiles):
    i = pl.program_id(0)

    @pl.when(i < n_tiles)
    def _():
        live = (i > 0).astype(F32)
        _seq1_compute(uh_ref[...] * live, u_ref, wdw_ref, bdw_ref, lng_ref, lnb_ref, o_ref, uext, vbuf,
                      tm=tm, width=width)

    @pl.when(i >= n_tiles)
    def _():
        o_ref[...] = jnp.zeros_like(o_ref)


def _seq1_sample_body(u_ref, uh_ref, wdw_ref, bdw_ref, lng_ref, lnb_ref, prev_ref, o_ref, uext, vbuf,
                      *, tm, width):
    del prev_ref
    _seq1_compute(uh_ref[...], u_ref, wdw_ref, bdw_ref, lng_ref, lnb_ref, o_ref, uext, vbuf,
                  tm=tm, width=width)


def _seq1(u, n_prompt, state_dwconv, w_dw, b_dw, ln_g, ln_b):
    t, d = u.shape
    n_batch = state_dwconv.shape[0]
    width = w_dw.shape[0]
    tm = _pick(n_prompt, (256, 128, 64))
    assert t % tm == 0
    consts = [w_dw, b_dw.reshape(1, d), ln_g.reshape(1, d), ln_b.reshape(1, d)]
    const_specs = [pl.BlockSpec(w_dw.shape, lambda i: (0, 0))] + [
        pl.BlockSpec((1, d), lambda i: (0, 0)) for _ in range(3)]
    scratch = lambda rows: [pltpu.VMEM((DWCONV_HIST_PAD + rows, d), F32), pltpu.VMEM((rows, d), F32)]
    v_p = pl.pallas_call(
        functools.partial(_seq1_prompt_body, tm=tm, width=width, n_tiles=n_prompt // tm),
        out_shape=jax.ShapeDtypeStruct((t, d), BF16),
        grid=(t // tm,),
        in_specs=[pl.BlockSpec((tm, d), lambda i: (i, 0)),
                  pl.BlockSpec((DWCONV_HIST_PAD, d),
                               lambda i: (jnp.maximum(i * (tm // DWCONV_HIST_PAD) - 1, 0), 0))]
        + const_specs,
        out_specs=pl.BlockSpec((tm, d), lambda i: (i, 0)),
        scratch_shapes=scratch(tm),
        compiler_params=_params("arbitrary"), name="seq1_prompt",
    )(u, u, *consts)
    c0 = n_prompt // CHUNK
    uh = jnp.pad(state_dwconv, ((0, 0), (DWCONV_HIST_PAD - state_dwconv.shape[1], 0), (0, 0)))
    return pl.pallas_call(
        functools.partial(_seq1_sample_body, tm=CHUNK, width=width),
        out_shape=jax.ShapeDtypeStruct((t, d), BF16),
        grid=(n_batch,),
        in_specs=[pl.BlockSpec((CHUNK, d), lambda i: (c0 + i, 0)),
                  pl.BlockSpec((None, DWCONV_HIST_PAD, d), lambda i: (i, 0, 0))] + const_specs
        + [pl.BlockSpec(memory_space=pl.ANY)],
        out_specs=pl.BlockSpec((CHUNK, d), lambda i: (c0 + i, 0)),
        scratch_shapes=scratch(CHUNK),
        input_output_aliases={6: 0},
        compiler_params=_params("arbitrary"), name="seq1_sample",
    )(u, uh, *consts, v_p)


def _ffn_body(te_ref, nv_ref, h_ref, wg_ref, wu_ref, wd_ref, *refs, n_chunk, n_cast):
    del te_ref
    cast_in, o_ref, cast_out = refs[:n_cast], refs[n_cast], refs[n_cast + 1:]
    i = pl.program_id(0)
    j = pl.program_id(1)

    @pl.when(j == 0)
    def _():
        o_ref[...] = jnp.zeros_like(o_ref)

    @pl.when(i < nv_ref[0])
    def _():
        for src, dst in zip(cast_in, cast_out):
            dst[...] = src[...].astype(dst.dtype)
        h = h_ref[...]
        g = jnp.dot(h, wg_ref[...], preferred_element_type=F32)
        u = jnp.dot(h, wu_ref[...], preferred_element_type=F32)
        a = (g * jax.nn.sigmoid(g) * u).astype(BF16)
        d = o_ref.shape[1]
        for c in range(d // n_chunk):
            cols = pl.ds(c * n_chunk, n_chunk)
            o_ref[:, cols] += jnp.dot(a, wd_ref[:, cols], preferred_element_type=F32)


def _ffn(h, w_gate, w_up, w_down, tile_set, n_valid, *, tm, tf, cast=()):
    p, d = h.shape
    f = w_gate.shape[2]
    n_tiles, n_f = p // tm, f // tf
    cast_specs, cast_shapes = _cast_blocks(cast, n_tiles * n_f, lambda i, j, te, nv: i * n_f + j)

    def row_map(i, j, te, nv):
        return (jnp.minimum(i, nv[0] - 1), 0)

    def _ij(i, j, te, nv):
        live = i < nv[0]
        return te[jnp.minimum(i, nv[0] - 1)], jnp.where(live, j, n_f - 1)

    def up_map(i, j, te, nv):
        e, jj = _ij(i, j, te, nv)
        return (e, 0, jj)

    def down_map(i, j, te, nv):
        e, jj = _ij(i, j, te, nv)
        return (e, jj, 0)

    grid_spec = pltpu.PrefetchScalarGridSpec(
        num_scalar_prefetch=2, grid=(n_tiles, n_f),
        in_specs=[pl.BlockSpec((tm, d), row_map),
                  pl.BlockSpec((None, d, tf), up_map),
                  pl.BlockSpec((None, d, tf), up_map),
                  pl.BlockSpec((None, tf, d), down_map)] + cast_specs,
        out_specs=[pl.BlockSpec((tm, d), lambda i, j, te, nv: (i, 0))] + cast_specs)
    return pl.pallas_call(
        functools.partial(_ffn_body, n_chunk=_pick(d, (1024, 512, 256, 128)), n_cast=len(cast)),
        out_shape=[jax.ShapeDtypeStruct((p, d), F32)] + cast_shapes, grid_spec=grid_spec,
        compiler_params=_params("arbitrary", "arbitrary"), name="ffn",
    )(tile_set, n_valid, h, w_gate, w_up, w_down, *cast)


def _row_copy(src_hbm, row, dst, r, sem):
    return pltpu.make_async_copy(src_hbm.at[pl.ds(row, 1), :], dst.at[pl.ds(r, 1), :], sem)


def _ring_step(n_steps, start_rows, wait_rows):
    i = pl.program_id(0)
    slot = lax.rem(i, RING_SLOTS)

    @pl.when(i == 0)
    def _():
        start_rows(False, 0)

    @pl.when(i + 1 < n_steps)
    def _():
        start_rows(True, 1 - slot)

    wait_rows(slot)
    return slot


def _gather_body(src_ref, src_next_ref, h_hbm, o_ref, buf, sem, *, rows, n_steps):
    def start_rows(is_next, slot):
        idx = src_next_ref if is_next else src_ref

        def start(r, c):
            _row_copy(h_hbm, idx[0, 0, r], buf.at[slot], r, sem.at[slot]).start()
            return c

        lax.fori_loop(0, rows, start, 0, unroll=DMA_UNROLL)

    def wait_rows(slot):
        def wait(r, c):
            _row_copy(h_hbm, 0, buf.at[slot], r, sem.at[slot]).wait()
            return c

        lax.fori_loop(0, rows, wait, 0, unroll=DMA_UNROLL)

    slot = _ring_step(n_steps, start_rows, wait_rows)
    half = buf.shape[2]
    o_ref[:, pl.ds(0, half)], o_ref[:, pl.ds(half, half)] = _unpack_bf16_pairs(buf[slot])


def _gather_rows(h, src, *, rows):
    d = 2 * h.shape[1]
    p = src.shape[0]
    n = p // rows
    src = src.reshape(n, 1, rows)
    idx_spec = lambda step: pl.BlockSpec((1, 1, rows), lambda i: (jnp.minimum(i + step, n - 1), 0, 0),
                                         memory_space=pltpu.SMEM)
    return pl.pallas_call(
        functools.partial(_gather_body, rows=rows, n_steps=n),
        out_shape=jax.ShapeDtypeStruct((p, d), BF16),
        grid=(n,),
        in_specs=[idx_spec(0), idx_spec(1), pl.BlockSpec(memory_space=pl.ANY)],
        out_specs=pl.BlockSpec((rows, d), lambda i: (i, 0)),
        scratch_shapes=[pltpu.VMEM((RING_SLOTS, rows, d // 2), jnp.uint32),
                        pltpu.SemaphoreType.DMA((RING_SLOTS,))],
        compiler_params=_params("arbitrary"), name="gather_rows",
    )(src, src, h)


def _combine_body(pos_ref, pos_next_ref, cw_ref, x_ref, gate_ref, g_ref, y_hbm, op_ref, os_ref, buf, sem,
                  *, rows, n_steps, n_prompt_tiles, n_prompt_chunks):
    i = pl.program_id(0)

    def start_rows(is_next, slot):
        idx = pos_next_ref if is_next else pos_ref

        def start(r, c):
            for k in range(TOP_K):
                _row_copy(y_hbm, idx[0, 0, TOP_K * r + k], buf.at[slot, k], r, sem.at[slot]).start()
            return c

        lax.fori_loop(0, rows, start, 0, unroll=DMA_UNROLL)

    def wait_rows(slot):
        def wait(r, c):
            for k in range(TOP_K):
                _row_copy(y_hbm, 0, buf.at[slot, k], r, sem.at[slot]).wait()
            return c

        lax.fori_loop(0, rows, wait, 0, unroll=DMA_UNROLL)

    slot = _ring_step(n_steps, start_rows, wait_rows)

    def emit(o_ref):
        for c in range(rows // CHUNK):
            rs = pl.ds(c * CHUNK, CHUNK)
            b = _chunk_batch(i * (rows // CHUNK) + c, n_prompt_chunks)
            cw = cw_ref[rs, :]
            f = cw[:, 0:1] * buf[slot, 0, rs, :] + cw[:, 1:2] * buf[slot, 1, rs, :]
            xv = x_ref[rs, :] + gate_ref[pl.ds(b, 1), :] * f
            ms = jnp.mean(xv * xv, axis=-1, keepdims=True)
            o_ref[rs, :] = xv * lax.rsqrt(ms + RMS_EPS) * g_ref[...]

    @pl.when(i < n_prompt_tiles)
    def _():
        emit(op_ref)

    @pl.when(i >= n_prompt_tiles)
    def _():
        emit(os_ref)


def _combine(x, y_sorted, pos, cw, mod, gate, g_final, n_prompt):
    t, d = x.shape
    rows = _pick(n_prompt, (256, 128, 64))
    n = t // rows
    npt = n_prompt // rows
    layer, gate_col = gate
    pos = pos.reshape(n, 1, TOP_K * rows)
    idx_spec = lambda step: pl.BlockSpec((1, 1, TOP_K * rows),
                                         lambda i: (jnp.minimum(i + step, n - 1), 0, 0),
                                         memory_space=pltpu.SMEM)
    return pl.pallas_call(
        functools.partial(_combine_body, rows=rows, n_steps=n, n_prompt_tiles=npt,
                          n_prompt_chunks=n_prompt // CHUNK),
        out_shape=[jax.ShapeDtypeStruct((n_prompt, d), F32),
                   jax.ShapeDtypeStruct((t - n_prompt, d), F32)],
        grid=(n,),
        in_specs=[idx_spec(0), idx_spec(1),
                  pl.BlockSpec((rows, ROUTE_LANES), lambda i: (i, 0)),
                  pl.BlockSpec((rows, d), lambda i: (i, 0)),
                  pl.BlockSpec((None, MOD_ROWS, d), lambda i: (layer, 0, gate_col)),
                  pl.BlockSpec((1, d), lambda i: (0, 0)),
                  pl.BlockSpec(memory_space=pl.ANY)],
        out_specs=[pl.BlockSpec((rows, d), lambda i: (jnp.minimum(i, npt - 1), 0)),
                   pl.BlockSpec((rows, d), lambda i: (jnp.maximum(i - npt, 0), 0))],
        scratch_shapes=[pltpu.VMEM((RING_SLOTS, TOP_K, rows, d), F32),
                        pltpu.SemaphoreType.DMA((RING_SLOTS,))],
        compiler_params=_params("arbitrary"), name="combine",
    )(pos, pos, cw, x, mod, g_final.reshape(1, d), y_sorted)


def _route_plan(e_idx, n_experts, tm, n_tiles):
    t = e_idx.shape[0]
    flat_e = e_idx.reshape(-1)
    onehot = (flat_e[:, None] == jnp.arange(n_experts)[None, :]).astype(jnp.int32)
    csum = jnp.cumsum(onehot, axis=0)
    rank = jnp.take_along_axis(csum, flat_e[:, None], axis=1)[:, 0] - 1
    counts = csum[-1]
    tiles_per = (counts + tm - 1) // tm
    tile_end = jnp.cumsum(tiles_per)
    row_start = (tile_end - tiles_per) * tm
    pos = row_start[flat_e] + rank
    src = jnp.zeros((n_tiles * tm,), jnp.int32).at[pos].set(jnp.arange(TOP_K * t, dtype=jnp.int32) // TOP_K)
    before = (tile_end[None, :] <= jnp.arange(n_tiles)[:, None]).astype(jnp.int32)
    tile_set = jnp.minimum(jnp.sum(before, axis=1), n_experts - 1).astype(jnp.int32)
    n_valid = tile_end[-1:].astype(jnp.int32)
    return pos.astype(jnp.int32).reshape(t, TOP_K), src, tile_set, n_valid


def kernel(x_prompt, x_sample, state_pool, state_shortconv, state_dwconv, c_prompt, c_sample, w_ada, b_ada, g_mix, g_ffn, g_final, w_in_ab, w_pool_grp, pool_scale, w_sconv, w_out_ab, w_ffn_gate, w_ffn_up, w_ffn_down, w_pw1, b_pw1, w_dw, b_dw, ln_g, ln_b, w_pw2, b_pw2, w_router, w_exp_gate, w_exp_up, w_exp_down):
    n_pb, seq, d = x_prompt.shape
    n_sb, dec_seq, _ = x_sample.shape
    assert n_pb == 1 and dec_seq == CHUNK and seq % CHUNK == 0
    assert 1 + n_sb <= MOD_ROWS and w_ada.shape[0] == 2
    n_prompt = seq
    t = n_prompt + n_sb * dec_seq
    npc = n_prompt // CHUNK
    d_pool = state_pool.shape[-1]
    n_experts = w_router.shape[-1]

    x = (x_prompt.reshape(n_prompt, d), x_sample.reshape(n_sb * dec_seq, d))
    c_all = jnp.concatenate([c_prompt, c_sample, jnp.zeros((MOD_ROWS - 1 - n_sb, d), F32)], axis=0)
    mod = _mod_tables(c_all, w_ada, b_ada)
    bf = lambda w: w.astype(BF16)

    (h,) = _addnorm(x, g_mix[0], npc, mod, shift=(0, 0), scale=(0, 1))
    z, wg0, wu0, wd0, w_out_b, w_pw1_b, w_pw2_b = _matmul(
        h, bf(w_in_ab[0]),
        cast=(w_ffn_gate[0], w_ffn_up[0], w_ffn_down[0], w_out_ab[0], w_pw1[0], w_pw2[0]))
    y_cat, utail_p, utail_s = _seq0(z, n_prompt, state_pool[0], state_shortconv[0],
                                    bf(w_pool_grp[0]), pool_scale[0], w_sconv[0])
    y = _matmul(y_cat, w_out_b)
    x, h = _addnorm(x, g_ffn[0], npc, mod, y=y, gate=(0, 2), shift=(0, 3), scale=(0, 4))
    tm_d = _pick(t, (512, 256, 128, 64))
    n_td = t // tm_d
    d_ffe = w_exp_gate.shape[3]
    f, wg_e, wu_e = _ffn(h, wg0[None], wu0[None], wd0[None],
                         jnp.zeros((n_td,), jnp.int32), jnp.full((1,), n_td, jnp.int32),
                         tm=tm_d, tf=_pick(w_ffn_gate.shape[2], (256, 128)),
                         cast=(w_exp_gate[0].reshape(n_experts * d, d_ffe),
                               w_exp_up[0].reshape(n_experts * d, d_ffe)))
    wg_e = wg_e.reshape(n_experts, d, d_ffe)
    wu_e = wu_e.reshape(n_experts, d, d_ffe)

    x, h = _addnorm(x, g_mix[1], npc, mod, y=f, gate=(0, 5), shift=(1, 0), scale=(1, 1))
    u = _matmul_glu(h, w_pw1_b, b_pw1[0])
    v = _seq1(u, n_prompt, state_dwconv[0], w_dw[0], b_dw[0], ln_g[0], ln_b[0])
    y = _matmul(v, w_pw2_b, b_pw2[0])
    x, h_rows, cw, e_idx = _addnorm(x, g_ffn[1], npc, mod, y=y, gate=(1, 2), shift=(1, 3), scale=(1, 4),
                                    h_dtypes=(jnp.uint32,), w_router=w_router[0])
    tm_e = _pick(TOP_K * t, (512, 256, 128, 64))
    n_te = TOP_K * t // tm_e + n_experts
    pos, src, tile_set, n_valid = _route_plan(e_idx[:, :TOP_K], n_experts, tm_e, n_te)
    h_sorted = _gather_rows(h_rows, src, rows=tm_e)
    (y_sorted,) = _ffn(h_sorted, wg_e, wu_e, bf(w_exp_down[0]), tile_set, n_valid,
                       tm=tm_e, tf=_pick(d_ffe, (512, 256, 128)))
    y_p, y_s = _combine(x, y_sorted, pos, cw, mod, (1, 5), g_final, n_prompt)

    n_tiles0 = utail_p.shape[0] // SCONV_HIST_PAD
    sconv_keep = state_shortconv.shape[2]
    dw_keep = state_dwconv.shape[2]
    zs = z[n_prompt:].reshape(n_sb, dec_seq, -1)
    us = u[n_prompt:].reshape(n_sb, dec_seq, -1)
    new_pool_p = z[n_prompt - POOL_HIST:n_prompt, :d_pool][None, None]
    new_pool_s = zs[:, dec_seq - POOL_HIST:, :d_pool][None]
    new_sconv_p = utail_p[(n_tiles0 - 1) * SCONV_HIST_PAD:][SCONV_HIST_PAD - sconv_keep:][None, None]
    new_sconv_s = utail_s.reshape(n_sb, SCONV_HIST_PAD, -1)[:, SCONV_HIST_PAD - sconv_keep:][None]
    new_dw_p = u[n_prompt - dw_keep:n_prompt][None, None]
    new_dw_s = us[:, dec_seq - dw_keep:][None]
    return (y_p.reshape(1, n_prompt, d), y_s.reshape(n_sb, dec_seq, d),
            new_pool_p, new_sconv_p, new_dw_p, new_pool_s, new_sconv_s, new_dw_s)
```

```python
import functools

import jax
import jax.numpy as jnp
from jax import lax
from jax.experimental import pallas as pl
from jax.experimental.pallas import tpu as pltpu

F32 = jnp.float32
BF16 = jnp.bfloat16

CHUNK = 64
POOL_WINDOWS = (2, 4, 8, 16)
POOL_HIST = max(POOL_WINDOWS) - 1
POOL_HIST_PAD = 16
SCONV_HIST_PAD = 8
DWCONV_HIST_PAD = 32
PAST_LEN = 4096
TOP_K = 2
RMS_EPS = 1e-6
LN_EPS = 1e-5
MOD_ROWS = 16
RING_SLOTS = 2
DMA_UNROLL = 8
SUBLANES = 8
BF16_SUBLANES = 16
LANES = 128
ROUTE_LANES = LANES
V7X_VMEM_BYTES = 64 * 1024 * 1024
VMEM_LIMIT = V7X_VMEM_BYTES - 6 * 1024 * 1024


def _params(*sem):
    return pltpu.CompilerParams(dimension_semantics=sem, vmem_limit_bytes=VMEM_LIMIT)


def _pick(n, prefs):
    for p in prefs:
        if n % p == 0:
            return p
    return n


def _mod_body(c_ref, w_ref, b_ref, o_ref):
    c = c_ref[...]
    sc = (c * jax.nn.sigmoid(c)).astype(BF16)
    o_ref[...] = jnp.dot(sc, w_ref[...].astype(BF16), preferred_element_type=F32) + b_ref[...]


def _mod_tables(c_all, w_ada, b_ada):
    depth, d, n = w_ada.shape
    tn = _pick(n, (512, 256, 128))
    return pl.pallas_call(
        _mod_body,
        out_shape=jax.ShapeDtypeStruct((depth, MOD_ROWS, n), F32),
        grid=(depth, n // tn),
        in_specs=[
            pl.BlockSpec((MOD_ROWS, d), lambda l, j: (0, 0)),
            pl.BlockSpec((None, d, tn), lambda l, j: (l, 0, j)),
            pl.BlockSpec((None, 1, tn), lambda l, j: (l, 0, j)),
        ],
        out_specs=pl.BlockSpec((None, MOD_ROWS, tn), lambda l, j: (l, 0, j)),
        compiler_params=_params("arbitrary", "arbitrary"),
        name="mod_tables",
    )(c_all, w_ada, b_ada.reshape(depth, 1, n))


def _chunk_batch(chunk_idx, n_prompt_chunks):
    return jnp.maximum(chunk_idx - (n_prompt_chunks - 1), 0)


HI16 = 0xFFFF0000


def _pack_bf16_pairs(v):
    half = v.shape[1] // 2
    bits = lax.bitcast_convert_type(v.astype(BF16).astype(F32), jnp.uint32)
    return (bits[:, :half] >> 16) | (bits[:, half:] & jnp.uint32(HI16))


def _unpack_bf16_pairs(w):
    lo = lax.bitcast_convert_type(w << 16, F32).astype(BF16)
    hi = lax.bitcast_convert_type(w & jnp.uint32(HI16), F32).astype(BF16)
    return lo, hi


def _top2(logits, n_experts):
    lane = lax.broadcasted_iota(jnp.int32, logits.shape, 1)
    lane_f = lane.astype(F32)
    neg = jnp.float32(-jnp.inf)
    lg = jnp.where(lane < n_experts, logits, neg)
    m1 = jnp.max(lg, axis=-1, keepdims=True)
    i1 = jnp.min(jnp.where(lg == m1, lane_f, float(ROUTE_LANES)), axis=-1, keepdims=True)
    lg2 = jnp.where(lane_f == i1, neg, lg)
    m2 = jnp.max(lg2, axis=-1, keepdims=True)
    i2 = jnp.min(jnp.where(lg2 == m2, lane_f, float(ROUTE_LANES)), axis=-1, keepdims=True)
    ex = jnp.exp(m2 - m1)
    den = 1.0 + ex
    weights = jnp.where(lane == 0, 1.0 / den, jnp.where(lane == 1, ex / den, 0.0))
    experts = jnp.where(lane == 0, i1, jnp.where(lane == 1, i2, 0.0)).astype(jnp.int32)
    return weights, experts


def _addnorm_body(*refs, tm, n_prompt_chunks, n_prompt_tiles, split_x, has_add, has_mod, n_experts):
    refs = list(refs)
    x_ref = refs.pop(0)
    xs_ref = refs.pop(0) if split_x else None
    if has_add:
        y_ref = refs.pop(0)
        gate_ref = refs.pop(0)
    g_ref = refs.pop(0)
    if has_mod:
        shift_ref = refs.pop(0)
        scale_ref = refs.pop(0)
    if n_experts:
        wr_ref = refs.pop(0)
    if has_add:
        xo_ref = refs.pop(0)
    if n_experts:
        re_ref = refs.pop()
        rw_ref = refs.pop()
    h_refs = refs
    i = pl.program_id(0)
    for c in range(tm // CHUNK):
        rows = pl.ds(c * CHUNK, CHUNK)
        b = _chunk_batch(i * (tm // CHUNK) + c, n_prompt_chunks)
        xv = x_ref[rows, :]
        if split_x:
            xv = jnp.where(i < n_prompt_tiles, xv, xs_ref[rows, :])
        if has_add:
            xv = xv + gate_ref[pl.ds(b, 1), :] * y_ref[rows, :]
            xo_ref[rows, :] = xv
        ms = jnp.mean(xv * xv, axis=-1, keepdims=True)
        hv = xv * lax.rsqrt(ms + RMS_EPS) * g_ref[...]
        if has_mod:
            hv = hv * (1.0 + scale_ref[pl.ds(b, 1), :]) + shift_ref[pl.ds(b, 1), :]
        for h_ref in h_refs:
            if h_ref.dtype == jnp.uint32:
                h_ref[rows, :] = _pack_bf16_pairs(hv)
            else:
                h_ref[rows, :] = hv.astype(h_ref.dtype)
        if n_experts:
            logits = jnp.dot(hv, wr_ref[...], preferred_element_type=F32,
                             precision=lax.Precision.HIGHEST)
            rw_ref[rows, :], re_ref[rows, :] = _top2(logits, n_experts)


def _addnorm(x, g, n_prompt_chunks, mod, *, y=None, gate=None, shift=None, scale=None,
             h_dtypes=(BF16,), w_router=None):
    split_x = isinstance(x, tuple)
    n_prompt = n_prompt_chunks * CHUNK
    t = sum(p.shape[0] for p in x) if split_x else x.shape[0]
    d = g.shape[0]
    tm = _pick(n_prompt, (256, 128, 64))
    assert t % tm == 0
    npt = n_prompt // tm
    has_add = y is not None
    has_mod = shift is not None
    n_experts = 0 if w_router is None else w_router.shape[1]
    row = pl.BlockSpec((tm, d), lambda i: (i, 0))

    def tab(where):
        layer, col = where
        return pl.BlockSpec((None, MOD_ROWS, d), lambda i: (layer, 0, col))

    if split_x:
        args = list(x)
        specs = [pl.BlockSpec((tm, d), lambda i: (jnp.minimum(i, npt - 1), 0)),
                 pl.BlockSpec((tm, d), lambda i: (jnp.maximum(i - npt, 0), 0))]
    else:
        args, specs = [x], [row]
    if has_add:
        args += [y, mod]
        specs += [row, tab(gate)]
    args.append(g.reshape(1, d))
    specs.append(pl.BlockSpec((1, d), lambda i: (0, 0)))
    if has_mod:
        args += [mod, mod]
        specs += [tab(shift), tab(scale)]
    if n_experts:
        args.append(jnp.pad(w_router, ((0, 0), (0, ROUTE_LANES - n_experts))))
        specs.append(pl.BlockSpec((d, ROUTE_LANES), lambda i: (0, 0)))
    out_shape, out_specs = [], []
    if has_add:
        out_shape.append(jax.ShapeDtypeStruct((t, d), F32))
        out_specs.append(row)
    for dt in h_dtypes:
        width = d // 2 if dt == jnp.uint32 else d
        out_shape.append(jax.ShapeDtypeStruct((t, width), dt))
        out_specs.append(pl.BlockSpec((tm, width), lambda i: (i, 0)))
    if n_experts:
        for dt in (F32, jnp.int32):
            out_shape.append(jax.ShapeDtypeStruct((t, ROUTE_LANES), dt))
            out_specs.append(pl.BlockSpec((tm, ROUTE_LANES), lambda i: (i, 0)))
    body = functools.partial(_addnorm_body, tm=tm, n_prompt_chunks=n_prompt_chunks, n_prompt_tiles=npt,
                             split_x=split_x, has_add=has_add, has_mod=has_mod, n_experts=n_experts)
    return pl.pallas_call(
        body, out_shape=out_shape, grid=(t // tm,), in_specs=specs, out_specs=out_specs,
        compiler_params=_params("arbitrary"), name="addnorm",
    )(*args)


def _cast_blocks(cast, n_steps, step_of):
    specs, shapes = [], []
    for w in cast:
        rows = w.shape[0]
        rb = next(r for r in range(BF16_SUBLANES, rows + 1, BF16_SUBLANES)
                  if rows % r == 0 and rows // r <= n_steps)
        nblk = rows // rb
        specs.append(pl.BlockSpec(
            (rb, w.shape[1]), lambda *ids, nblk=nblk: (jnp.minimum(step_of(*ids), nblk - 1), 0)))
        shapes.append(jax.ShapeDtypeStruct(w.shape, BF16))
    return specs, shapes


def _mm_body(a_ref, b_ref, *refs, has_bias, n_cast):
    refs = list(refs)
    bias_ref = refs.pop(0) if has_bias else None
    cast_in, o_ref, cast_out = refs[:n_cast], refs[n_cast], refs[n_cast + 1:]
    for src, dst in zip(cast_in, cast_out):
        dst[...] = src[...].astype(dst.dtype)
    acc = jnp.dot(a_ref[...], b_ref[...], preferred_element_type=F32)
    if has_bias:
        acc = acc + bias_ref[...]
    o_ref[...] = acc


def _matmul(a, b, bias=None, cast=()):
    t, k = a.shape
    n = b.shape[1]
    tm = _pick(t, (768, 512, 256, 128, 64))
    tn = _pick(n, (1024, 512, 256, 128))
    n_j = n // tn
    args = [a, b]
    specs = [pl.BlockSpec((tm, k), lambda i, j: (i, 0)), pl.BlockSpec((k, tn), lambda i, j: (0, j))]
    if bias is not None:
        args.append(bias.reshape(1, n))
        specs.append(pl.BlockSpec((1, tn), lambda i, j: (0, j)))
    cast_specs, cast_shapes = _cast_blocks(cast, (t // tm) * n_j, lambda i, j: i * n_j + j)
    out = pl.pallas_call(
        functools.partial(_mm_body, has_bias=bias is not None, n_cast=len(cast)),
        out_shape=[jax.ShapeDtypeStruct((t, n), F32)] + cast_shapes,
        grid=(t // tm, n_j), in_specs=specs + cast_specs,
        out_specs=[pl.BlockSpec((tm, tn), lambda i, j: (i, j))] + cast_specs,
        compiler_params=_params("arbitrary", "arbitrary"), name="matmul",
    )(*args, *cast)
    return out if cast else out[0]


def _glu_body(a_ref, ba_ref, bg_ref, bias_a_ref, bias_g_ref, *refs, n_cast):
    cast_in, o_ref, cast_out = refs[:n_cast], refs[n_cast], refs[n_cast + 1:]
    for src, dst in zip(cast_in, cast_out):
        dst[...] = src[...].astype(dst.dtype)
    a = a_ref[...]
    za = jnp.dot(a, ba_ref[...], preferred_element_type=F32) + bias_a_ref[...]
    zg = jnp.dot(a, bg_ref[...], preferred_element_type=F32) + bias_g_ref[...]
    o_ref[...] = za * jax.nn.sigmoid(zg)


def _matmul_glu(a, b, bias, cast=()):
    t, k = a.shape
    n = b.shape[1] // 2
    tm = _pick(t, (768, 512, 256, 128, 64))
    tn = _pick(n, (256, 128) if cast else (512, 256, 128))
    nb = n // tn
    bias = bias.reshape(1, 2 * n)
    cast_specs, cast_shapes = _cast_blocks(cast, (t // tm) * nb, lambda i, j: i * nb + j)
    out = pl.pallas_call(
        functools.partial(_glu_body, n_cast=len(cast)),
        out_shape=[jax.ShapeDtypeStruct((t, n), F32)] + cast_shapes,
        grid=(t // tm, nb),
        in_specs=[
            pl.BlockSpec((tm, k), lambda i, j: (i, 0)),
            pl.BlockSpec((k, tn), lambda i, j: (0, j)),
            pl.BlockSpec((k, tn), lambda i, j: (0, j + nb)),
            pl.BlockSpec((1, tn), lambda i, j: (0, j)),
            pl.BlockSpec((1, tn), lambda i, j: (0, j + nb)),
        ] + cast_specs,
        out_specs=[pl.BlockSpec((tm, tn), lambda i, j: (i, j))] + cast_specs,
        compiler_params=_params("arbitrary", "arbitrary"), name="matmul_glu",
    )(a, b, b, bias, bias, *cast)
    return out if cast else out[0]


def _seq0_compute(pos_base, ph, uh, p_ref, bg_ref, cg_ref, v_ref, wgrp_ref, scale_ref, wsc_ref,
                  o_ref, utail_ref, pext, uext, *, tm, d_pool):
    n_groups = len(POOL_WINDOWS)
    gw = d_pool // n_groups
    pext[pl.ds(0, POOL_HIST_PAD), :] = ph
    pext[pl.ds(POOL_HIST_PAD, tm), :] = p_ref[...]
    u = cg_ref[...] * v_ref[...]
    uext[pl.ds(0, SCONV_HIST_PAD), :] = uh
    uext[pl.ds(SCONV_HIST_PAD, tm), :] = u
    utail_ref[...] = u[tm - SCONV_HIST_PAD:, :]

    pos = pos_base + lax.broadcasted_iota(jnp.int32, (tm, 1), 0)
    for gi, w in enumerate(POOL_WINDOWS):
        cols = pl.ds(gi * gw, gw)
        cur = pext[pl.ds(POOL_HIST_PAD, tm), cols]
        s = cur
        for k in range(1, w):
            s = s + pext[pl.ds(POOL_HIST_PAD - k, tm), cols]
        cnt = jnp.minimum(w, pos + 1).astype(F32)
        dlt = (s / cnt - cur).astype(BF16)
        ya = jnp.dot(dlt, wgrp_ref[gi], preferred_element_type=F32) * scale_ref[:, cols]
        o_ref[:, cols] = ya.astype(o_ref.dtype)

    conv = (uext[pl.ds(SCONV_HIST_PAD - 2, tm), :] * wsc_ref[pl.ds(0, 1), :]
            + uext[pl.ds(SCONV_HIST_PAD - 1, tm), :] * wsc_ref[pl.ds(1, 1), :]
            + uext[pl.ds(SCONV_HIST_PAD, tm), :] * wsc_ref[pl.ds(2, 1), :])
    o_ref[:, pl.ds(d_pool, d_pool)] = (bg_ref[...] * conv).astype(o_ref.dtype)


def _seq0_prompt_body(p_ref, bg_ref, cg_ref, v_ref, ph_ref, cgh_ref, vh_ref, wgrp_ref, scale_ref,
                      wsc_ref, o_ref, utail_ref, pext, uext, *, tm, d_pool, n_tiles):
    i = pl.program_id(0)

    @pl.when(i < n_tiles)
    def _():
        live = (i > 0).astype(F32)
        ph = ph_ref[...] * live
        uh = cgh_ref[...] * vh_ref[...] * live
        _seq0_compute(i * tm, ph, uh, p_ref, bg_ref, cg_ref, v_ref, wgrp_ref, scale_ref, wsc_ref,
                      o_ref, utail_ref, pext, uext, tm=tm, d_pool=d_pool)

    @pl.when(i >= n_tiles)
    def _():
        o_ref[...] = jnp.zeros_like(o_ref)


def _seq0_sample_body(p_ref, bg_ref, cg_ref, v_ref, ph_ref, uh_ref, wgrp_ref, scale_ref, wsc_ref,
                      prev_ref, o_ref, utail_ref, pext, uext, *, tm, d_pool):
    del prev_ref
    _seq0_compute(PAST_LEN, ph_ref[...], uh_ref[...], p_ref, bg_ref, cg_ref, v_ref, wgrp_ref,
                  scale_ref, wsc_ref, o_ref, utail_ref, pext, uext, tm=tm, d_pool=d_pool)


def _seq0(z, n_prompt, state_pool, state_sconv, w_grp, pool_scale, w_sconv):
    t = z.shape[0]
    n_batch, _, d_pool = state_pool.shape
    tm = _pick(n_prompt, (256, 128, 64))
    assert t % tm == 0
    n_tiles = n_prompt // tm
    scratch = lambda rows: [pltpu.VMEM((POOL_HIST_PAD + rows, d_pool), F32),
                            pltpu.VMEM((SCONV_HIST_PAD + rows, d_pool), F32)]
    consts = [w_grp, pool_scale.reshape(1, d_pool), w_sconv]
    const_specs = [pl.BlockSpec(w_grp.shape, lambda i: (0, 0, 0)),
                   pl.BlockSpec((1, d_pool), lambda i: (0, 0)),
                   pl.BlockSpec(w_sconv.shape, lambda i: (0, 0))]
    col = lambda c: pl.BlockSpec((tm, d_pool), lambda i: (i, c))
    hist = lambda rows, c: pl.BlockSpec(
        (rows, d_pool), lambda i: (jnp.maximum(i * (tm // rows) - 1, 0), c))
    y_p, utail_p = pl.pallas_call(
        functools.partial(_seq0_prompt_body, tm=tm, d_pool=d_pool, n_tiles=n_tiles),
        out_shape=[jax.ShapeDtypeStruct((t, 2 * d_pool), BF16),
                   jax.ShapeDtypeStruct((n_tiles * SCONV_HIST_PAD, d_pool), F32)],
        grid=(t // tm,),
        in_specs=[col(0), col(1), col(2), col(3), hist(POOL_HIST_PAD, 0),
                  hist(SCONV_HIST_PAD, 2), hist(SCONV_HIST_PAD, 3)] + const_specs,
        out_specs=[pl.BlockSpec((tm, 2 * d_pool), lambda i: (i, 0)),
                   pl.BlockSpec((SCONV_HIST_PAD, d_pool), lambda i: (jnp.minimum(i, n_tiles - 1), 0))],
        scratch_shapes=scratch(tm),
        compiler_params=_params("arbitrary"), name="seq0_prompt",
    )(z, z, z, z, z, z, z, *consts)

    c0 = n_prompt // CHUNK
    ph = jnp.pad(state_pool, ((0, 0), (POOL_HIST_PAD - state_pool.shape[1], 0), (0, 0)))
    uh = jnp.pad(state_sconv, ((0, 0), (SCONV_HIST_PAD - state_sconv.shape[1], 0), (0, 0)))
    scol = lambda c: pl.BlockSpec((CHUNK, d_pool), lambda i: (c0 + i, c))
    y, utail_s = pl.pallas_call(
        functools.partial(_seq0_sample_body, tm=CHUNK, d_pool=d_pool),
        out_shape=[jax.ShapeDtypeStruct((t, 2 * d_pool), BF16),
                   jax.ShapeDtypeStruct((n_batch * SCONV_HIST_PAD, d_pool), F32)],
        grid=(n_batch,),
        in_specs=[scol(0), scol(1), scol(2), scol(3),
                  pl.BlockSpec((None, POOL_HIST_PAD, d_pool), lambda i: (i, 0, 0)),
                  pl.BlockSpec((None, SCONV_HIST_PAD, d_pool), lambda i: (i, 0, 0))] + const_specs
        + [pl.BlockSpec(memory_space=pl.ANY)],
        out_specs=[pl.BlockSpec((CHUNK, 2 * d_pool), lambda i: (c0 + i, 0)),
                   pl.BlockSpec((SCONV_HIST_PAD, d_pool), lambda i: (i, 0))],
        scratch_shapes=scratch(CHUNK),
        input_output_aliases={9: 0},
        compiler_params=_params("arbitrary"), name="seq0_sample",
    )(z, z, z, z, ph, uh, *consts, y_p)
    return y, utail_p, utail_s


def _seq1_compute(uh, u_ref, wdw_ref, bdw_ref, lng_ref, lnb_ref, o_ref, uext, vbuf, *, tm, width):
    d = u_ref.shape[1]
    uext[pl.ds(0, DWCONV_HIST_PAD), :] = uh
    uext[pl.ds(DWCONV_HIST_PAD, tm), :] = u_ref[...]
    base = DWCONV_HIST_PAD - (width - 1)
    assert base >= 0 and width - 1 + base <= DWCONV_HIST_PAD
    cw = LANES
    rb = _pick(tm, (128, 64))

    def col_block(cb, carry):
        cols = pl.ds(pl.multiple_of(cb * cw, cw), cw)
        for r in range(tm // rb):
            acc = jnp.broadcast_to(bdw_ref[:, cols], (rb, cw))
            for res in range(SUBLANES):
                taps = [k for k in range(width) if (base + k) % SUBLANES == res]
                n = rb if res == 0 else rb + SUBLANES
                part = None
                for k in taps:
                    term = uext[pl.ds(r * rb + base + k - res, n), cols] * wdw_ref[pl.ds(k, 1), cols]
                    part = term if part is None else part + term
                if part is not None:
                    acc = acc + part[res:res + rb]
            vbuf[pl.ds(r * rb, rb), cols] = acc
        return carry

    lax.fori_loop(0, d // cw, col_block, 0)
    rb = CHUNK
    for r in range(tm // rb):
        rows = pl.ds(r * rb, rb)
        v = vbuf[rows, :]
        mu = jnp.mean(v, axis=-1, keepdims=True)
        vc = v - mu
        var = jnp.mean(vc * vc, axis=-1, keepdims=True)
        y = vc * lax.rsqrt(var + LN_EPS) * lng_ref[...] + lnb_ref[...]
        o_ref[rows, :] = (y * jax.nn.sigmoid(y)).astype(o_ref.dtype)


def _seq1_prompt_body(u_ref, uh_ref, wdw_ref, bdw_ref, lng_ref, lnb_ref, o_ref, uext, vbuf,
                      *, tm, width, n_tiles):
    i = pl.program_id(0)

    @pl.when(i < n_tiles)
    def _():
        live = (i > 0).astype(F32)
        _seq1_compute(uh_ref[...] * live, u_ref, wdw_ref, bdw_ref, lng_ref, lnb_ref, o_ref, uext, vbuf,
                      tm=tm, width=width)

    @pl.when(i >= n_tiles)
    def _():
        o_ref[...] = jnp.zeros_like(o_ref)


def _seq1_sample_body(u_ref, uh_ref, wdw_ref, bdw_ref, lng_ref, lnb_ref, prev_ref, o_ref, uext, vbuf,
                      *, tm, width):
    del prev_ref
    _seq1_compute(uh_ref[...], u_ref, wdw_ref, bdw_ref, lng_ref, lnb_ref, o_ref, uext, vbuf,
                  tm=tm, width=width)


def _seq1(u, n_prompt, state_dwconv, w_dw, b_dw, ln_g, ln_b):
    t, d = u.shape
    n_batch = state_dwconv.shape[0]
    width = w_dw.shape[0]
    tm = _pick(n_prompt, (256, 128, 64))
    assert t % tm == 0
    consts = [w_dw, b_dw.reshape(1, d), ln_g.reshape(1, d), ln_b.reshape(1, d)]
    const_specs = [pl.BlockSpec(w_dw.shape, lambda i: (0, 0))] + [
        pl.BlockSpec((1, d), lambda i: (0, 0)) for _ in range(3)]
    scratch = lambda rows: [pltpu.VMEM((DWCONV_HIST_PAD + rows, d), F32), pltpu.VMEM((rows, d), F32)]
    v_p = pl.pallas_call(
        functools.partial(_seq1_prompt_body, tm=tm, width=width, n_tiles=n_prompt // tm),
        out_shape=jax.ShapeDtypeStruct((t, d), BF16),
        grid=(t // tm,),
        in_specs=[pl.BlockSpec((tm, d), lambda i: (i, 0)),
                  pl.BlockSpec((DWCONV_HIST_PAD, d),
                               lambda i: (jnp.maximum(i * (tm // DWCONV_HIST_PAD) - 1, 0), 0))]
        + const_specs,
        out_specs=pl.BlockSpec((tm, d), lambda i: (i, 0)),
        scratch_shapes=scratch(tm),
        compiler_params=_params("arbitrary"), name="seq1_prompt",
    )(u, u, *consts)
    c0 = n_prompt // CHUNK
    uh = jnp.pad(state_dwconv, ((0, 0), (DWCONV_HIST_PAD - state_dwconv.shape[1], 0), (0, 0)))
    return pl.pallas_call(
        functools.partial(_seq1_sample_body, tm=CHUNK, width=width),
        out_shape=jax.ShapeDtypeStruct((t, d), BF16),
        grid=(n_batch,),
        in_specs=[pl.BlockSpec((CHUNK, d), lambda i: (c0 + i, 0)),
                  pl.BlockSpec((None, DWCONV_HIST_PAD, d), lambda i: (i, 0, 0))] + const_specs
        + [pl.BlockSpec(memory_space=pl.ANY)],
        out_specs=pl.BlockSpec((CHUNK, d), lambda i: (c0 + i, 0)),
        scratch_shapes=scratch(CHUNK),
        input_output_aliases={6: 0},
        compiler_params=_params("arbitrary"), name="seq1_sample",
    )(u, uh, *consts, v_p)


def _ffn_body(te_ref, nv_ref, h_ref, wg_ref, wu_ref, wd_ref, *refs, n_chunk, n_cast):
    del te_ref
    cast_in, o_ref, cast_out = refs[:n_cast], refs[n_cast], refs[n_cast + 1:]
    i = pl.program_id(0)
    j = pl.program_id(1)

    @pl.when(j == 0)
    def _():
        o_ref[...] = jnp.zeros_like(o_ref)

    @pl.when(i < nv_ref[0])
    def _():
        for src, dst in zip(cast_in, cast_out):
            dst[...] = src[...].astype(dst.dtype)
        h = h_ref[...]
        g = jnp.dot(h, wg_ref[...], preferred_element_type=F32)
        u = jnp.dot(h, wu_ref[...], preferred_element_type=F32)
        a = (g * jax.nn.sigmoid(g) * u).astype(BF16)
        d = o_ref.shape[1]
        for c in range(d // n_chunk):
            cols = pl.ds(c * n_chunk, n_chunk)
            o_ref[:, cols] += jnp.dot(a, wd_ref[:, cols], preferred_element_type=F32)


def _ffn(h, w_gate, w_up, w_down, tile_set, n_valid, *, tm, tf, cast=()):
    p, d = h.shape
    f = w_gate.shape[2]
    n_tiles, n_f = p // tm, f // tf
    cast_specs, cast_shapes = _cast_blocks(cast, n_tiles * n_f, lambda i, j, te, nv: i * n_f + j)

    def row_map(i, j, te, nv):
        return (jnp.minimum(i, nv[0] - 1), 0)

    def _ij(i, j, te, nv):
        live = i < nv[0]
        return te[jnp.minimum(i, nv[0] - 1)], jnp.where(live, j, n_f - 1)

    def up_map(i, j, te, nv):
        e, jj = _ij(i, j, te, nv)
        return (e, 0, jj)

    def down_map(i, j, te, nv):
        e, jj = _ij(i, j, te, nv)
        return (e, jj, 0)

    grid_spec = pltpu.PrefetchScalarGridSpec(
        num_scalar_prefetch=2, grid=(n_tiles, n_f),
        in_specs=[pl.BlockSpec((tm, d), row_map),
                  pl.BlockSpec((None, d, tf), up_map),
                  pl.BlockSpec((None, d, tf), up_map),
                  pl.BlockSpec((None, tf, d), down_map)] + cast_specs,
        out_specs=[pl.BlockSpec((tm, d), lambda i, j, te, nv: (i, 0))] + cast_specs)
    return pl.pallas_call(
        functools.partial(_ffn_body, n_chunk=_pick(d, (1024, 512, 256, 128)), n_cast=len(cast)),
        out_shape=[jax.ShapeDtypeStruct((p, d), F32)] + cast_shapes, grid_spec=grid_spec,
        compiler_params=_params("arbitrary", "arbitrary"), name="ffn",
    )(tile_set, n_valid, h, w_gate, w_up, w_down, *cast)


def _row_copy(src_hbm, row, dst, r, sem):
    return pltpu.make_async_copy(src_hbm.at[pl.ds(row, 1), :], dst.at[pl.ds(r, 1), :], sem)


def _ring_step(n_steps, start_rows, wait_rows):
    i = pl.program_id(0)
    slot = lax.rem(i, RING_SLOTS)

    @pl.when(i == 0)
    def _():
        start_rows(False, 0)

    @pl.when(i + 1 < n_steps)
    def _():
        start_rows(True, 1 - slot)

    wait_rows(slot)
    return slot


def _gather_body(src_ref, src_next_ref, h_hbm, o_ref, buf, sem, *, rows, n_steps):
    def start_rows(is_next, slot):
        idx = src_next_ref if is_next else src_ref

        def start(r, c):
            _row_copy(h_hbm, idx[0, 0, r], buf.at[slot], r, sem.at[slot]).start()
            return c

        lax.fori_loop(0, rows, start, 0, unroll=DMA_UNROLL)

    def wait_rows(slot):
        def wait(r, c):
            _row_copy(h_hbm, 0, buf.at[slot], r, sem.at[slot]).wait()
            return c

        lax.fori_loop(0, rows, wait, 0, unroll=DMA_UNROLL)

    slot = _ring_step(n_steps, start_rows, wait_rows)
    half = buf.shape[2]
    o_ref[:, pl.ds(0, half)], o_ref[:, pl.ds(half, half)] = _unpack_bf16_pairs(buf[slot])


def _gather_rows(h, src, *, rows):
    d = 2 * h.shape[1]
    p = src.shape[0]
    n = p // rows
    src = src.reshape(n, 1, rows)
    idx_spec = lambda step: pl.BlockSpec((1, 1, rows), lambda i: (jnp.minimum(i + step, n - 1), 0, 0),
                                         memory_space=pltpu.SMEM)
    return pl.pallas_call(
        functools.partial(_gather_body, rows=rows, n_steps=n),
        out_shape=jax.ShapeDtypeStruct((p, d), BF16),
        grid=(n,),
        in_specs=[idx_spec(0), idx_spec(1), pl.BlockSpec(memory_space=pl.ANY)],
        out_specs=pl.BlockSpec((rows, d), lambda i: (i, 0)),
        scratch_shapes=[pltpu.VMEM((RING_SLOTS, rows, d // 2), jnp.uint32),
                        pltpu.SemaphoreType.DMA((RING_SLOTS,))],
        compiler_params=_params("arbitrary"), name="gather_rows",
    )(src, src, h)


def _combine_body(pos_ref, pos_next_ref, cw_ref, x_ref, gate_ref, g_ref, y_hbm, op_ref, os_ref, buf, sem,
                  *, rows, n_steps, n_prompt_tiles, n_prompt_chunks):
    i = pl.program_id(0)

    def start_rows(is_next, slot):
        idx = pos_next_ref if is_next else pos_ref

        def start(r, c):
            for k in range(TOP_K):
                _row_copy(y_hbm, idx[0, 0, TOP_K * r + k], buf.at[slot, k], r, sem.at[slot]).start()
            return c

        lax.fori_loop(0, rows, start, 0, unroll=DMA_UNROLL)

    def wait_rows(slot):
        def wait(r, c):
            for k in range(TOP_K):
                _row_copy(y_hbm, 0, buf.at[slot, k], r, sem.at[slot]).wait()
            return c

        lax.fori_loop(0, rows, wait, 0, unroll=DMA_UNROLL)

    slot = _ring_step(n_steps, start_rows, wait_rows)

    def emit(o_ref):
        for c in range(rows // CHUNK):
            rs = pl.ds(c * CHUNK, CHUNK)
            b = _chunk_batch(i * (rows // CHUNK) + c, n_prompt_chunks)
            cw = cw_ref[rs, :]
            f = cw[:, 0:1] * buf[slot, 0, rs, :] + cw[:, 1:2] * buf[slot, 1, rs, :]
            xv = x_ref[rs, :] + gate_ref[pl.ds(b, 1), :] * f
            ms = jnp.mean(xv * xv, axis=-1, keepdims=True)
            o_ref[rs, :] = xv * lax.rsqrt(ms + RMS_EPS) * g_ref[...]

    @pl.when(i < n_prompt_tiles)
    def _():
        emit(op_ref)

    @pl.when(i >= n_prompt_tiles)
    def _():
        emit(os_ref)


def _combine(x, y_sorted, pos, cw, mod, gate, g_final, n_prompt):
    t, d = x.shape
    rows = _pick(n_prompt, (256, 128, 64))
    n = t // rows
    npt = n_prompt // rows
    layer, gate_col = gate
    pos = pos.reshape(n, 1, TOP_K * rows)
    idx_spec = lambda step: pl.BlockSpec((1, 1, TOP_K * rows),
                                         lambda i: (jnp.minimum(i + step, n - 1), 0, 0),
                                         memory_space=pltpu.SMEM)
    return pl.pallas_call(
        functools.partial(_combine_body, rows=rows, n_steps=n, n_prompt_tiles=npt,
                          n_prompt_chunks=n_prompt // CHUNK),
        out_shape=[jax.ShapeDtypeStruct((n_prompt, d), F32),
                   jax.ShapeDtypeStruct((t - n_prompt, d), F32)],
        grid=(n,),
        in_specs=[idx_spec(0), idx_spec(1),
                  pl.BlockSpec((rows, ROUTE_LANES), lambda i: (i, 0)),
                  pl.BlockSpec((rows, d), lambda i: (i, 0)),
                  pl.BlockSpec((None, MOD_ROWS, d), lambda i: (layer, 0, gate_col)),
                  pl.BlockSpec((1, d), lambda i: (0, 0)),
                  pl.BlockSpec(memory_space=pl.ANY)],
        out_specs=[pl.BlockSpec((rows, d), lambda i: (jnp.minimum(i, npt - 1), 0)),
                   pl.BlockSpec((rows, d), lambda i: (jnp.maximum(i - npt, 0), 0))],
        scratch_shapes=[pltpu.VMEM((RING_SLOTS, TOP_K, rows, d), F32),
                        pltpu.SemaphoreType.DMA((RING_SLOTS,))],
        compiler_params=_params("arbitrary"), name="combine",
    )(pos, pos, cw, x, mod, g_final.reshape(1, d), y_sorted)


def _route_plan(e_idx, n_experts, tm, n_tiles):
    t = e_idx.shape[0]
    flat_e = e_idx.reshape(-1)
    onehot = (flat_e[:, None] == jnp.arange(n_experts)[None, :]).astype(jnp.int32)
    csum = jnp.cumsum(onehot, axis=0)
    rank = jnp.take_along_axis(csum, flat_e[:, None], axis=1)[:, 0] - 1
    counts = csum[-1]
    tiles_per = (counts + tm - 1) // tm
    tile_end = jnp.cumsum(tiles_per)
    row_start = (tile_end - tiles_per) * tm
    pos = row_start[flat_e] + rank
    src = jnp.zeros((n_tiles * tm,), jnp.int32).at[pos].set(jnp.arange(TOP_K * t, dtype=jnp.int32) // TOP_K)
    before = (tile_end[None, :] <= jnp.arange(n_tiles)[:, None]).astype(jnp.int32)
    tile_set = jnp.minimum(jnp.sum(before, axis=1), n_experts - 1).astype(jnp.int32)
    n_valid = tile_end[-1:].astype(jnp.int32)
    return pos.astype(jnp.int32).reshape(t, TOP_K), src, tile_set, n_valid


def kernel(x_prompt, x_sample, state_pool, state_shortconv, state_dwconv, c_prompt, c_sample, w_ada, b_ada, g_mix, g_ffn, g_final, w_in_ab, w_pool_grp, pool_scale, w_sconv, w_out_ab, w_ffn_gate, w_ffn_up, w_ffn_down, w_pw1, b_pw1, w_dw, b_dw, ln_g, ln_b, w_pw2, b_pw2, w_router, w_exp_gate, w_exp_up, w_exp_down):
    n_pb, seq, d = x_prompt.shape
    n_sb, dec_seq, _ = x_sample.shape
    assert n_pb == 1 and dec_seq == CHUNK and seq % CHUNK == 0
    assert 1 + n_sb <= MOD_ROWS and w_ada.shape[0] == 2
    n_prompt = seq
    t = n_prompt + n_sb * dec_seq
    npc = n_prompt // CHUNK
    d_pool = state_pool.shape[-1]
    n_experts = w_router.shape[-1]

    x = (x_prompt.reshape(n_prompt, d), x_sample.reshape(n_sb * dec_seq, d))
    c_all = jnp.concatenate([c_prompt, c_sample, jnp.zeros((MOD_ROWS - 1 - n_sb, d), F32)], axis=0)
    mod = _mod_tables(c_all, w_ada, b_ada)
    bf = lambda w: w.astype(BF16)

    (h,) = _addnorm(x, g_mix[0], npc, mod, shift=(0, 0), scale=(0, 1))
    z, wg0, wu0, wd0, w_out_b, w_pw1_b, w_pw2_b = _matmul(
        h, bf(w_in_ab[0]),
        cast=(w_ffn_gate[0], w_ffn_up[0], w_ffn_down[0], w_out_ab[0], w_pw1[0], w_pw2[0]))
    y_cat, utail_p, utail_s = _seq0(z, n_prompt, state_pool[0], state_shortconv[0],
                                    bf(w_pool_grp[0]), pool_scale[0], w_sconv[0])
    y = _matmul(y_cat, w_out_b)
    x, h = _addnorm(x, g_ffn[0], npc, mod, y=y, gate=(0, 2), shift=(0, 3), scale=(0, 4))
    tm_d = _pick(t, (512, 256, 128, 64))
    n_td = t // tm_d
    d_ffe = w_exp_gate.shape[3]
    f, wg_e, wu_e = _ffn(h, wg0[None], wu0[None], wd0[None],
                         jnp.zeros((n_td,), jnp.int32), jnp.full((1,), n_td, jnp.int32),
                         tm=tm_d, tf=_pick(w_ffn_gate.shape[2], (256, 128)),
                         cast=(w_exp_gate[0].reshape(n_experts * d, d_ffe),
                               w_exp_up[0].reshape(n_experts * d, d_ffe)))
    wg_e = wg_e.reshape(n_experts, d, d_ffe)
    wu_e = wu_e.reshape(n_experts, d, d_ffe)

    x, h = _addnorm(x, g_mix[1], npc, mod, y=f, gate=(0, 5), shift=(1, 0), scale=(1, 1))
    u, wd_e = _matmul_glu(h, w_pw1_b, b_pw1[0], cast=(w_exp_down[0].reshape(n_experts * d_ffe, d),))
    wd_e = wd_e.reshape(n_experts, d_ffe, d)
    v = _seq1(u, n_prompt, state_dwconv[0], w_dw[0], b_dw[0], ln_g[0], ln_b[0])
    y = _matmul(v, w_pw2_b, b_pw2[0])
    x, h_rows, cw, e_idx = _addnorm(x, g_ffn[1], npc, mod, y=y, gate=(1, 2), shift=(1, 3), scale=(1, 4),
                                    h_dtypes=(jnp.uint32,), w_router=w_router[0])
    tm_e = _pick(TOP_K * t, (512, 256, 128, 64))
    n_te = TOP_K * t // tm_e + n_experts
    pos, src, tile_set, n_valid = _route_plan(e_idx[:, :TOP_K], n_experts, tm_e, n_te)
    h_sorted = _gather_rows(h_rows, src, rows=tm_e)
    (y_sorted,) = _ffn(h_sorted, wg_e, wu_e, wd_e, tile_set, n_valid,
                       tm=tm_e, tf=_pick(d_ffe, (512, 256, 128)))
    y_p, y_s = _combine(x, y_sorted, pos, cw, mod, (1, 5), g_final, n_prompt)

    n_tiles0 = utail_p.shape[0] // SCONV_HIST_PAD
    sconv_keep = state_shortconv.shape[2]
    dw_keep = state_dwconv.shape[2]
    zs = z[n_prompt:].reshape(n_sb, dec_seq, -1)
    us = u[n_prompt:].reshape(n_sb, dec_seq, -1)
    new_pool_p = z[n_prompt - POOL_HIST:n_prompt, :d_pool][None, None]
    new_pool_s = zs[:, dec_seq - POOL_HIST:, :d_pool][None]
    new_sconv_p = utail_p[(n_tiles0 - 1) * SCONV_HIST_PAD:][SCONV_HIST_PAD - sconv_keep:][None, None]
    new_sconv_s = utail_s.reshape(n_sb, SCONV_HIST_PAD, -1)[:, SCONV_HIST_PAD - sconv_keep:][None]
    new_dw_p = u[n_prompt - dw_keep:n_prompt][None, None]
    new_dw_s = us[:, dec_seq - dw_keep:][None]
    return (y_p.reshape(1, n_prompt, d), y_s.reshape(n_sb, dec_seq, d),
            new_pool_p, new_sconv_p, new_dw_p, new_pool_s, new_sconv_s, new_dw_s)
```
